```python
import math
import jax, jax.numpy as jnp
from jax import lax
import numpy as np

D_MODEL = 1024
BATCH = 8
SEQ = 8192
DEPTH = 2

HEAD_DIM = 64
MIX_WIDTH = D_MODEL
N_HEADS_A = (MIX_WIDTH // 2) // HEAD_DIM
N_KV_A = N_HEADS_A // 4
N_HEADS_B = (MIX_WIDTH // 2) // (2 * HEAD_DIM)
A_Q = N_HEADS_A * HEAD_DIM
A_KV = N_KV_A * HEAD_DIM
B_QK = N_HEADS_B * 2 * HEAD_DIM
B_V = N_HEADS_B * 2 * HEAD_DIM
IN_WIDTH = A_Q + 2 * A_KV + 2 * B_QK + B_V
D_FF = 7 * D_MODEL // 2
N_EXPERTS = 8
TOP_K = 2
EXPERT_BLOCK = 512
Q_BLOCK = 128
GRID_W = 64
ROPE_THETA = 10000.0
EPS = 1e-6
N_DENSE = (DEPTH + 1) // 2
N_MOE = DEPTH // 2

kernel_name = "hybrid_gqa_axial_diffattn_moe_encoder"


def rms_norm(x, g):
    xf = x.astype(jnp.float32)
    y = xf * lax.rsqrt(jnp.mean(xf * xf, axis=-1, keepdims=True) + EPS)
    return (y * g.astype(jnp.float32)).astype(x.dtype)


def rope_freqs(pos, dim):
    inv = ROPE_THETA ** (-jnp.arange(0, dim, 2, dtype=jnp.float32) / dim)
    ang = pos.astype(jnp.float32)[:, None] * inv[None, :]
    return jnp.cos(ang), jnp.sin(ang)


def apply_rope(x, cos, sin):
    half = x.shape[-1] // 2
    shape = (1, x.shape[1]) + (1,) * (x.ndim - 3) + (half,)
    c = cos.reshape(shape).astype(x.dtype)
    s = sin.reshape(shape).astype(x.dtype)
    x1, x2 = x[..., :half], x[..., half:]
    return jnp.concatenate([x1 * c - x2 * s, x2 * c + x1 * s], axis=-1)


def apply_axial_rope(x, row_cs, col_cs):
    half = x.shape[-1] // 2
    return jnp.concatenate([apply_rope(x[..., :half], *row_cs),
                            apply_rope(x[..., half:], *col_cs)], axis=-1)


def gqa_attention(q, k, v):
    b, s, hq, d = q.shape
    hkv = k.shape[2]
    g = hq // hkv
    nb = s // Q_BLOCK
    scale = d ** -0.5
    qb = jnp.moveaxis(q.reshape(b, nb, Q_BLOCK, hkv, g, d), 1, 0)

    def block(qblk):
        sc = jnp.einsum('bqhgd,bkhd->bhgqk', qblk, k, preferred_element_type=jnp.float32) * scale
        p = jax.nn.softmax(sc, axis=-1).astype(v.dtype)
        return jnp.einsum('bhgqk,bkhd->bqhgd', p, v)

    o = lax.map(block, qb)
    return jnp.moveaxis(o, 0, 1).reshape(b, s, hq * d)


def diff_attention(q, k, v, lam):
    b, s, h, _, d = q.shape
    nb = s // Q_BLOCK
    scale = d ** -0.5
    qb = jnp.moveaxis(q.reshape(b, nb, Q_BLOCK, h, 2, d), 1, 0)

    def block(qblk):
        sc = jnp.einsum('bqhmd,bkhmd->bhmqk', qblk, k, preferred_element_type=jnp.float32) * scale
        p = jax.nn.softmax(sc, axis=-1)
        a = (p[:, :, 0] - lam * p[:, :, 1]).astype(v.dtype)
        return jnp.einsum('bhqk,bkhe->bqhe', a, v)

    o = lax.map(block, qb)
    return jnp.moveaxis(o, 0, 1).reshape(b, s, h, v.shape[-1])


def swiglu(x, w_gate, w_up, w_down):
    return (jax.nn.silu(x @ w_gate) * (x @ w_up)) @ w_down


def moe_swiglu(x2d, w_router, w_gate, w_up, w_down):
    n, d = x2d.shape
    logits = jnp.matmul(x2d, w_router, preferred_element_type=jnp.float32)
    top_logit, top_e = lax.top_k(logits, TOP_K)
    top_w = jax.nn.softmax(top_logit, axis=-1)
    nk = n * TOP_K
    flat_e = top_e.reshape(nk)
    flat_tok = jnp.repeat(jnp.arange(n, dtype=jnp.int32), TOP_K)
    flat_w = top_w.reshape(nk)
    order = jnp.argsort(flat_e, stable=True)
    sorted_e = flat_e[order]
    counts = jnp.zeros((N_EXPERTS,), jnp.int32).at[flat_e].add(1)
    padded = ((counts + EXPERT_BLOCK - 1) // EXPERT_BLOCK) * EXPERT_BLOCK
    pad_end = jnp.cumsum(padded)
    pad_start = pad_end - padded
    grp_start = jnp.cumsum(counts) - counts
    rank = jnp.arange(nk, dtype=jnp.int32) - grp_start[sorted_e]
    dest = pad_start[sorted_e] + rank
    n_blocks = -(-nk // EXPERT_BLOCK) + N_EXPERTS
    cap = n_blocks * EXPERT_BLOCK
    slot_tok = jnp.zeros((cap,), jnp.int32).at[dest].set(flat_tok[order])
    slot_w = jnp.zeros((cap,), jnp.float32).at[dest].set(flat_w[order])
    block_start = jnp.arange(n_blocks, dtype=jnp.int32) * EXPERT_BLOCK
    block_expert = jnp.minimum(jnp.searchsorted(pad_end, block_start, side='right'), N_EXPERTS - 1)
    xs = x2d[slot_tok].reshape(n_blocks, EXPERT_BLOCK, d)

    def expert_block(args):
        xb, e = args
        return swiglu(xb, w_gate[e], w_up[e], w_down[e])

    ys = lax.map(expert_block, (xs, block_expert)).reshape(cap, d)
    ys = ys * slot_w[:, None].astype(ys.dtype)
    return jnp.zeros((n, d), x2d.dtype).at[slot_tok].add(ys)


def setup_inputs(seed: int = 0) -> dict:
    key = jax.random.key(seed)
    ks = jax.random.split(key, 20)
    f32 = jnp.float32

    def nrm(k, shape, scale):
        return jax.random.normal(k, shape, f32) * scale

    def gain(k, shape):
        return 1.0 + 0.05 * jax.random.normal(k, shape, f32)

    return {
        "x": jax.random.normal(ks[0], (BATCH, SEQ, D_MODEL), f32),
        "w_in": nrm(ks[1], (DEPTH, D_MODEL, IN_WIDTH), D_MODEL ** -0.5),
        "w_out": nrm(ks[2], (DEPTH, MIX_WIDTH, D_MODEL), MIX_WIDTH ** -0.5),
        "g_pre_mix": gain(ks[3], (DEPTH, D_MODEL)),
        "g_post_mix": gain(ks[4], (DEPTH, D_MODEL)),
        "g_pre_ffn": gain(ks[5], (DEPTH, D_MODEL)),
        "g_post_ffn": gain(ks[6], (DEPTH, D_MODEL)),
        "g_qnorm_a": gain(ks[7], (DEPTH, HEAD_DIM)),
        "g_knorm_a": gain(ks[8], (DEPTH, HEAD_DIM)),
        "g_out_a": gain(ks[9], (DEPTH, A_Q)),
        "diff_lambda": nrm(ks[10], (DEPTH, 4, HEAD_DIM), 0.1),
        "g_subln_b": gain(ks[11], (DEPTH, 2 * HEAD_DIM)),
        "w_gate_dense": nrm(ks[12], (N_DENSE, D_MODEL, D_FF), D_MODEL ** -0.5),
        "w_up_dense": nrm(ks[13], (N_DENSE, D_MODEL, D_FF), D_MODEL ** -0.5),
        "w_down_dense": nrm(ks[14], (N_DENSE, D_FF, D_MODEL), D_FF ** -0.5),
        "w_router": nrm(ks[15], (N_MOE, D_MODEL, N_EXPERTS), D_MODEL ** -0.5),
        "w_gate_moe": nrm(ks[16], (N_MOE, N_EXPERTS, D_MODEL, D_FF), D_MODEL ** -0.5),
        "w_up_moe": nrm(ks[17], (N_MOE, N_EXPERTS, D_MODEL, D_FF), D_MODEL ** -0.5),
        "w_down_moe": nrm(ks[18], (N_MOE, N_EXPERTS, D_FF, D_MODEL), D_FF ** -0.5),
    }


def reference(x, w_in, w_out, g_pre_mix, g_post_mix, g_pre_ffn, g_post_ffn,
              g_qnorm_a, g_knorm_a, g_out_a, diff_lambda, g_subln_b,
              w_gate_dense, w_up_dense, w_down_dense,
              w_router, w_gate_moe, w_up_moe, w_down_moe):
    b, s, d = x.shape
    rows = s // GRID_W
    row_idx = jnp.repeat(jnp.arange(rows, dtype=jnp.int32), GRID_W)
    col_idx = jnp.tile(jnp.arange(GRID_W, dtype=jnp.int32), rows)
    row_cs = rope_freqs(row_idx, HEAD_DIM // 2)
    col_cs = rope_freqs(col_idx, HEAD_DIM // 2)
    seq_cs = rope_freqs(jnp.arange(s, dtype=jnp.int32), HEAD_DIM)
    splits = np.cumsum([A_Q, A_KV, A_KV, B_QK, B_QK]).tolist()

    h = x
    for i in range(DEPTH):
        hn = rms_norm(h, g_pre_mix[i])
        proj = hn @ w_in[i]
        qa, ka, va, qb, kb, vb = jnp.split(proj, splits, axis=-1)

        qa = rms_norm(qa.reshape(b, s, N_HEADS_A, HEAD_DIM), g_qnorm_a[i])
        ka = rms_norm(ka.reshape(b, s, N_KV_A, HEAD_DIM), g_knorm_a[i])
        qa = apply_axial_rope(qa, row_cs, col_cs)
        ka = apply_axial_rope(ka, row_cs, col_cs)
        va = va.reshape(b, s, N_KV_A, HEAD_DIM)
        oa = rms_norm(gqa_attention(qa, ka, va), g_out_a[i])

        lam_init = 0.8 - 0.6 * math.exp(-0.3 * i)
        lp = diff_lambda[i].astype(jnp.float32)
        lam = jnp.exp(jnp.sum(lp[0] * lp[1])) - jnp.exp(jnp.sum(lp[2] * lp[3])) + lam_init
        qb = apply_rope(qb.reshape(b, s, N_HEADS_B, 2, HEAD_DIM), *seq_cs)
        kb = apply_rope(kb.reshape(b, s, N_HEADS_B, 2, HEAD_DIM), *seq_cs)
        vb = vb.reshape(b, s, N_HEADS_B, 2 * HEAD_DIM)
        ob = diff_attention(qb, kb, vb, lam)
        ob = (rms_norm(ob, g_subln_b[i]) * (1.0 - lam_init)).reshape(b, s, B_V)

        mix = jnp.concatenate([oa, ob], axis=-1) @ w_out[i]
        h = h + rms_norm(mix, g_post_mix[i])

        hn = rms_norm(h, g_pre_ffn[i])
        if i % 2 == 0:
            j = i // 2
            f = swiglu(hn, w_gate_dense[j], w_up_dense[j], w_down_dense[j])
        else:
            j = i // 2
            f = moe_swiglu(hn.reshape(b * s, d), w_router[j], w_gate_moe[j],
                           w_up_moe[j], w_down_moe[j]).reshape(b, s, d)
        h = h + rms_norm(f, g_post_ffn[i])
    return h
```

```python
import functools
import math

import jax
import jax.numpy as jnp
from jax import lax
from jax.experimental import pallas as pl
from jax.experimental.pallas import tpu as pltpu

F32 = jnp.float32
BF16 = jnp.bfloat16

HEAD_DIM = 64
N_HEADS_A = 8
N_KV_A = 2
N_HEADS_B = 4
A_Q = N_HEADS_A * HEAD_DIM
A_KV = N_KV_A * HEAD_DIM
B_QK = N_HEADS_B * 2 * HEAD_DIM
B_V = N_HEADS_B * 2 * HEAD_DIM
GRID_W = 64
ROPE_THETA = 10000.0
EPS = 1e-6
N_EXPERTS = 8
LOG2E = 1.4426950408889634
LANES = 128
Q_SCALE = (HEAD_DIM ** -0.5) * LOG2E

VMEM_LIMIT = 56 * 1024 * 1024


def _cparams(sem):
    return pltpu.CompilerParams(dimension_semantics=sem, vmem_limit_bytes=VMEM_LIMIT)


def _rms(x, g):
    return x * lax.rsqrt(jnp.mean(x * x, axis=-1, keepdims=True) + EPS) * g


def _lane(shape):
    return lax.broadcasted_iota(jnp.int32, shape, len(shape) - 1)


def _tile4(t):
    return jnp.concatenate([t, t, t, t], axis=1)


def _rotate_pairs(x, half):
    w = x.shape[-1]
    first = (_lane(x.shape) % (2 * half)) < half
    return jnp.where(first, pltpu.roll(x, w - half, axis=1), pltpu.roll(x, half, axis=1))


def _inproj_kernel(h_ref, g_ref, w_ref, gq_ref, gk_ref, ca_ref, sa_ref, cb_ref, sb_ref, bd_ref,
                   qa_ref, ka_ref, va_ref, qb_ref, kb_ref, vb_ref):
    hn = _rms(h_ref[...], g_ref[...]).astype(BF16)
    tm = hn.shape[0]
    lane = _lane((tm, LANES))
    low = lane < HEAD_DIM
    ca, sa = ca_ref[...], sa_ref[...]
    cb, sb = cb_ref[...], sb_ref[...]

    def proj(lo, hi):
        return jnp.dot(hn, w_ref[:, lo:hi], preferred_element_type=F32)

    def head_norm(x, bd, g):
        ms = jnp.dot((x * x).astype(BF16), bd, preferred_element_type=F32) * (1.0 / HEAD_DIM)
        return x * lax.rsqrt(ms + EPS) * g

    qa = head_norm(proj(0, A_Q), bd_ref[...], gq_ref[...])
    qa = qa * _tile4(ca) + _rotate_pairs(qa, HEAD_DIM // 4) * _tile4(sa)
    for i in range(N_HEADS_A // 2):
        slab = qa[:, i * LANES:(i + 1) * LANES]
        qa_ref[:, (2 * i) * LANES:(2 * i + 1) * LANES] = jnp.where(low, slab, 0.0).astype(BF16)
        qa_ref[:, (2 * i + 1) * LANES:(2 * i + 2) * LANES] = jnp.where(
            low, pltpu.roll(slab, HEAD_DIM, axis=1), 0.0).astype(BF16)

    ka = head_norm(proj(A_Q, A_Q + A_KV), bd_ref[0:LANES, 0:LANES], gk_ref[...])
    ka = ka * ca + _rotate_pairs(ka, HEAD_DIM // 4) * sa
    ka_ref[:, 0:LANES] = jnp.where(low, ka, 0.0).astype(BF16)
    ka_ref[:, LANES:2 * LANES] = jnp.where(low, pltpu.roll(ka, HEAD_DIM, axis=1), 0.0).astype(BF16)

    va = proj(A_Q + A_KV, A_Q + 2 * A_KV)
    one_col = jnp.where(lane == HEAD_DIM, 1.0, 0.0)
    va_ref[:, 0:LANES] = jnp.where(low, va, one_col).astype(BF16)
    va_ref[:, LANES:2 * LANES] = jnp.where(low, pltpu.roll(va, HEAD_DIM, axis=1), one_col).astype(BF16)

    o = A_Q + 2 * A_KV
    qb = proj(o, o + B_QK)
    qb = (qb * _tile4(cb) + _rotate_pairs(qb, HEAD_DIM // 2) * _tile4(sb)) * Q_SCALE
    qb_ref[...] = qb.astype(BF16)
    kb = proj(o + B_QK, o + 2 * B_QK)
    kb = kb * _tile4(cb) + _rotate_pairs(kb, HEAD_DIM // 2) * _tile4(sb)
    kb_ref[...] = kb.astype(BF16)
    vb = proj(o + 2 * B_QK, o + 2 * B_QK + B_V)
    first_col = jnp.where(lane == 0, 1.0, 0.0).astype(BF16)
    for hd in range(N_HEADS_B):
        vb_ref[:, (2 * hd) * LANES:(2 * hd + 1) * LANES] = vb[:, hd * LANES:(hd + 1) * LANES].astype(BF16)
        vb_ref[:, (2 * hd + 1) * LANES:(2 * hd + 2) * LANES] = first_col


def _inproj(h2d, g, w_bf, gq, gk, tabs, bd, seq, tm):
    n, d = h2d.shape
    nt = seq // tm
    ca, sa, cb, sb = tabs
    row = lambda i: (i, 0)
    const = lambda i: (0, 0)
    tab = lambda i: (i % nt, 0)
    widths = (N_HEADS_A * LANES, N_KV_A * LANES, N_KV_A * LANES, B_QK, B_QK, 2 * B_V)
    return pl.pallas_call(
        _inproj_kernel,
        grid=(n // tm,),
        in_specs=[pl.BlockSpec((tm, d), row), pl.BlockSpec((1, d), const),
                  pl.BlockSpec(w_bf.shape, const), pl.BlockSpec((1, A_Q), const),
                  pl.BlockSpec((1, A_KV), const)]
                 + [pl.BlockSpec((tm, LANES), tab)] * 4
                 + [pl.BlockSpec(bd.shape, const)],
        out_specs=[pl.BlockSpec((tm, w), row) for w in widths],
        out_shape=[jax.ShapeDtypeStruct((n, w), BF16) for w in widths],
        compiler_params=_cparams(("parallel",)),
        name="inproj",
    )(h2d, g, w_bf, gq, gk, ca, sa, cb, sb, bd)


def _flash(q, k_ref, v_ref, tk):
    m_rows = q.shape[0]
    nk = k_ref.shape[0] // tk

    def body(c, carry):
        m, acc = carry
        off = pl.multiple_of(c * tk, tk)
        s = lax.dot_general(q, k_ref[pl.ds(off, tk), :], (((1,), (1,)), ((), ())),
                            preferred_element_type=F32)
        m_new = jnp.maximum(m, jnp.max(s, axis=-1, keepdims=True))
        p = jnp.exp2(s - m_new).astype(BF16)
        acc = jnp.exp2(m - m_new) * acc + jnp.dot(p, v_ref[pl.ds(off, tk), :],
                                                  preferred_element_type=F32)
        return m_new, acc

    m0 = jnp.full((m_rows, 1), -jnp.inf, F32)
    acc0 = jnp.zeros((m_rows, v_ref.shape[1]), F32)
    return lax.fori_loop(0, nk, body, (m0, acc0))[1]


def _attn_a_kernel(q_ref, k_ref, v_ref, o_ref, *, tk):
    tq = q_ref.shape[0]
    g = N_HEADS_A // N_KV_A
    q = jnp.concatenate([q_ref[:, j * LANES:(j + 1) * LANES] for j in range(g)], axis=0)
    acc = _flash(q, k_ref, v_ref, tk)
    o = acc * (1.0 / acc[:, HEAD_DIM:HEAD_DIM + 1])
    low = _lane((tq, LANES)) < HEAD_DIM
    slabs = [jnp.where(low, o[(2 * i) * tq:(2 * i + 1) * tq],
                       pltpu.roll(o[(2 * i + 1) * tq:(2 * i + 2) * tq], HEAD_DIM, axis=1))
             for i in range(g // 2)]
    o_ref[...] = jnp.concatenate(slabs, axis=1).astype(o_ref.dtype)


def _attn_a(qa, ka, va, tq, tk):
    b, s, _ = qa.shape
    g = N_HEADS_A // N_KV_A
    return pl.pallas_call(
        functools.partial(_attn_a_kernel, tk=tk),
        grid=(b, N_KV_A, s // tq),
        in_specs=[pl.BlockSpec((None, tq, g * LANES), lambda bi, h, i: (bi, i, h)),
                  pl.BlockSpec((None, s, LANES), lambda bi, h, i: (bi, 0, h)),
                  pl.BlockSpec((None, s, LANES), lambda bi, h, i: (bi, 0, h))],
        out_specs=pl.BlockSpec((None, tq, g * HEAD_DIM), lambda bi, h, i: (bi, i, h)),
        out_shape=jax.ShapeDtypeStruct((b, s, A_Q), BF16),
        compiler_params=_cparams(("parallel", "parallel", "parallel")),
        name="attn_a",
    )(qa, ka, va)


def _attn_b_kernel(lam_ref, gs_ref, q_ref, k_ref, v_ref, o_ref, *, tk, lam_init):
    tq = q_ref.shape[0]
    q = q_ref[...]
    low = _lane((tq, LANES)) < HEAD_DIM
    zero = jnp.zeros_like(q)
    qs = jnp.concatenate([jnp.where(low, q, zero), jnp.where(low, zero, q)], axis=0)
    acc = _flash(qs, k_ref, v_ref, tk)
    w = 2 * HEAD_DIM
    o = acc[:, 0:w] * (1.0 / acc[:, w:w + 1])
    lp = lam_ref[...]
    lam = (jnp.exp(jnp.sum(lp[0:1] * lp[1:2], axis=-1, keepdims=True))
           - jnp.exp(jnp.sum(lp[2:3] * lp[3:4], axis=-1, keepdims=True)) + lam_init)
    d = o[0:tq] - lam * o[tq:2 * tq]
    o_ref[...] = (_rms(d, gs_ref[...]) * (1.0 - lam_init)).astype(o_ref.dtype)


def _attn_b(lam_p, gs, qb, kb, vb, tq, tk, lam_init):
    b, s, _ = qb.shape
    return pl.pallas_call(
        functools.partial(_attn_b_kernel, tk=tk, lam_init=lam_init),
        grid=(b, N_HEADS_B, s // tq),
        in_specs=[pl.BlockSpec(lam_p.shape, lambda bi, h, i: (0, 0)),
                  pl.BlockSpec(gs.shape, lambda bi, h, i: (0, 0)),
                  pl.BlockSpec((None, tq, LANES), lambda bi, h, i: (bi, i, h)),
                  pl.BlockSpec((None, s, LANES), lambda bi, h, i: (bi, 0, h)),
                  pl.BlockSpec((None, s, 2 * LANES), lambda bi, h, i: (bi, 0, h))],
        out_specs=pl.BlockSpec((None, tq, LANES), lambda bi, h, i: (bi, i, h)),
        out_shape=jax.ShapeDtypeStruct((b, s, B_V), BF16),
        compiler_params=_cparams(("parallel", "parallel", "parallel")),
        name="attn_b",
    )(lam_p, gs, qb, kb, vb)


def _outproj_kernel(oa_ref, ob_ref, h_ref, w_ref, ga_ref, gpost_ref, gpre_ref, h2_ref, hn_ref):
    a = _rms(oa_ref[...].astype(F32), ga_ref[...]).astype(BF16)
    mix = (jnp.dot(a, w_ref[0:A_Q, :], preferred_element_type=F32)
           + jnp.dot(ob_ref[...], w_ref[A_Q:, :], preferred_element_type=F32))
    h2 = h_ref[...] + _rms(mix, gpost_ref[...])
    h2_ref[...] = h2
    hn_ref[...] = _rms(h2, gpre_ref[...]).astype(hn_ref.dtype)


def _outproj(oa, ob, h2d, w_bf, ga, gpost, gpre, hn_dtype, tm):
    n, d = h2d.shape
    row = lambda i: (i, 0)
    const = lambda i: (0, 0)
    return pl.pallas_call(
        _outproj_kernel,
        grid=(n // tm,),
        in_specs=[pl.BlockSpec((tm, A_Q), row), pl.BlockSpec((tm, B_V), row),
                  pl.BlockSpec((tm, d), row), pl.BlockSpec(w_bf.shape, const),
                  pl.BlockSpec((1, A_Q), const), pl.BlockSpec((1, d), const),
                  pl.BlockSpec((1, d), const)],
        out_specs=[pl.BlockSpec((tm, d), row), pl.BlockSpec((tm, d), row)],
        out_shape=[jax.ShapeDtypeStruct((n, d), F32), jax.ShapeDtypeStruct((n, d), hn_dtype)],
        compiler_params=_cparams(("parallel",)),
        name="outproj",
    )(oa, ob, h2d, w_bf, ga, gpost, gpre)


def _swiglu_step(x, wg_ref, wu_ref, wd_ref, acc_ref):
    j = pl.program_id(1)

    @pl.when(j == 0)
    def _():
        acc_ref[...] = jnp.zeros_like(acc_ref)

    gate = jnp.dot(x, wg_ref[...], preferred_element_type=F32)
    up = jnp.dot(x, wu_ref[...], preferred_element_type=F32)
    act = (gate * jax.nn.sigmoid(gate) * up).astype(BF16)
    acc_ref[...] += jnp.dot(act, wd_ref[...], preferred_element_type=F32)


def _ffn_dense_kernel(x_ref, wg_ref, wu_ref, wd_ref, h_ref, g_ref, o_ref, acc_ref):
    _swiglu_step(x_ref[...], wg_ref, wu_ref, wd_ref, acc_ref)

    @pl.when(pl.program_id(1) == pl.num_programs(1) - 1)
    def _():
        o_ref[...] = h_ref[...] + _rms(acc_ref[...], g_ref[...])


def _ffn_dense(x_bf, wg, wu, wd, h2d, g, tm, tf):
    n, d = h2d.shape
    f = wg.shape[1]
    return pl.pallas_call(
        _ffn_dense_kernel,
        grid=(n // tm, f // tf),
        in_specs=[pl.BlockSpec((tm, d), lambda i, j: (i, 0)),
                  pl.BlockSpec((d, tf), lambda i, j: (0, j)),
                  pl.BlockSpec((d, tf), lambda i, j: (0, j)),
                  pl.BlockSpec((tf, d), lambda i, j: (j, 0)),
                  pl.BlockSpec((tm, d), lambda i, j: (i, 0)),
                  pl.BlockSpec((1, d), lambda i, j: (0, 0))],
        out_specs=pl.BlockSpec((tm, d), lambda i, j: (i, 0)),
        out_shape=jax.ShapeDtypeStruct((n, d), F32),
        scratch_shapes=[pltpu.VMEM((tm, d), F32)],
        compiler_params=_cparams(("parallel", "arbitrary")),
        name="ffn_dense",
    )(x_bf, wg, wu, wd, h2d, g)


def _ffn_moe_kernel(be_ref, x_ref, wg_ref, wu_ref, wd_ref, o_ref, acc_ref):
    del be_ref
    _swiglu_step(x_ref[...].astype(BF16), wg_ref, wu_ref, wd_ref, acc_ref)

    @pl.when(pl.program_id(1) == pl.num_programs(1) - 1)
    def _():
        o_ref[...] = acc_ref[...]


def _ffn_moe(block_expert, xs, wg, wu, wd, tb, tf):
    cap, d = xs.shape
    f = wg.shape[2]
    grid_spec = pltpu.PrefetchScalarGridSpec(
        num_scalar_prefetch=1,
        grid=(cap // tb, f // tf),
        in_specs=[pl.BlockSpec((tb, d), lambda i, j, be: (i, 0)),
                  pl.BlockSpec((None, d, tf), lambda i, j, be: (be[i], 0, j)),
                  pl.BlockSpec((None, d, tf), lambda i, j, be: (be[i], 0, j)),
                  pl.BlockSpec((None, tf, d), lambda i, j, be: (be[i], j, 0))],
        out_specs=pl.BlockSpec((tb, d), lambda i, j, be: (i, 0)),
        scratch_shapes=[pltpu.VMEM((tb, d), F32)],
    )
    return pl.pallas_call(
        _ffn_moe_kernel,
        grid_spec=grid_spec,
        out_shape=jax.ShapeDtypeStruct((cap, d), F32),
        compiler_params=_cparams(("parallel", "arbitrary")),
        name="ffn_moe",
    )(block_expert, xs, wg, wu, wd)


def _route_kernel(x_ref, wr_ref, tri_ref, oi_ref, ow_ref, cnt_ref, carry_ref):
    @pl.when(pl.program_id(0) == 0)
    def _():
        carry_ref[...] = jnp.zeros_like(carry_ref)

    logits = lax.dot_general(wr_ref[...], x_ref[...], (((1,), (1,)), ((), ())),
                             precision=lax.Precision.HIGHEST, preferred_element_type=F32)
    shape = logits.shape
    eidx = lax.broadcasted_iota(jnp.int32, shape, 0)
    m1 = jnp.max(logits, axis=0, keepdims=True)
    i1 = jnp.min(jnp.where(logits == m1, eidx, N_EXPERTS), axis=0, keepdims=True)
    oh1 = eidx == i1
    rest = jnp.where(oh1, -jnp.inf, logits)
    m2 = jnp.max(rest, axis=0, keepdims=True)
    i2 = jnp.min(jnp.where(rest == m2, eidx, N_EXPERTS), axis=0, keepdims=True)
    oh2 = eidx == i2
    e = jnp.exp(m2 - m1)
    w1 = 1.0 / (1.0 + e)
    w2 = e * w1

    cnt = jnp.where(oh1, 1.0, jnp.where(oh2, 1.0, 0.0))
    incl = jnp.dot(cnt.astype(BF16), tri_ref[...], preferred_element_type=F32)
    excl = incl - cnt + carry_ref[:, 0:1]
    r1 = jnp.sum(jnp.where(oh1, excl, 0.0), axis=0, keepdims=True).astype(jnp.int32)
    r2 = jnp.sum(jnp.where(oh2, excl, 0.0), axis=0, keepdims=True).astype(jnp.int32)
    carry_ref[...] = carry_ref[...] + jnp.sum(cnt, axis=1, keepdims=True)

    bc = lambda v: jnp.broadcast_to(v, shape)
    oi_ref[...] = jnp.where(eidx == 0, bc(i1), jnp.where(eidx == 1, bc(i2),
                            jnp.where(eidx == 2, bc(r1), bc(r2))))
    ow_ref[...] = jnp.where(eidx == 0, bc(w1), bc(w2))
    cnt_ref[...] = carry_ref[...].astype(jnp.int32)


def _route(hn_f32, wr_t, tr):
    n, d = hn_f32.shape
    tri = (lax.broadcasted_iota(jnp.int32, (tr, tr), 0)
           <= lax.broadcasted_iota(jnp.int32, (tr, tr), 1)).astype(BF16)
    return pl.pallas_call(
        _route_kernel,
        grid=(n // tr,),
        in_specs=[pl.BlockSpec((tr, d), lambda i: (i, 0)),
                  pl.BlockSpec(wr_t.shape, lambda i: (0, 0)),
                  pl.BlockSpec((tr, tr), lambda i: (0, 0))],
        out_specs=[pl.BlockSpec((N_EXPERTS, tr), lambda i: (0, i)),
                   pl.BlockSpec((N_EXPERTS, tr), lambda i: (0, i)),
                   pl.BlockSpec((N_EXPERTS, LANES), lambda i: (0, 0))],
        out_shape=[jax.ShapeDtypeStruct((N_EXPERTS, n), jnp.int32),
                   jax.ShapeDtypeStruct((N_EXPERTS, n), F32),
                   jax.ShapeDtypeStruct((N_EXPERTS, LANES), jnp.int32)],
        scratch_shapes=[pltpu.VMEM((N_EXPERTS, LANES), F32)],
        compiler_params=_cparams(("arbitrary",)),
        name="route",
    )(hn_f32, wr_t, tri)


def _row_copy(src_hbm, dst_ref, sem, src_row, dst_row):
    return pltpu.make_async_copy(src_hbm.at[pl.ds(src_row, 1), :], dst_ref.at[pl.ds(dst_row, 1), :], sem)


def _gather_rows(idx_ref, src_hbm, dst_ref, sem):
    rows = dst_ref.shape[0]

    def start(r, c):
        _row_copy(src_hbm, dst_ref, sem, idx_ref[0, 0, r], r).start()
        return c

    def wait(r, c):
        _row_copy(src_hbm, dst_ref, sem, 0, r).wait()
        return c

    lax.fori_loop(0, rows, start, 0)
    lax.fori_loop(0, rows, wait, 0)


def _gather_kernel(idx_ref, src_hbm, o_ref, sem):
    _gather_rows(idx_ref, src_hbm, o_ref, sem.at[0])


def _gather(slot_tok, src, rows):
    cap = slot_tok.shape[0]
    d = src.shape[1]
    idx3 = slot_tok.reshape(cap // rows, 1, rows)
    return pl.pallas_call(
        _gather_kernel,
        grid=(cap // rows,),
        in_specs=[pl.BlockSpec((1, 1, rows), lambda i: (i, 0, 0), memory_space=pltpu.SMEM),
                  pl.BlockSpec(memory_space=pl.ANY)],
        out_specs=pl.BlockSpec((rows, d), lambda i: (i, 0)),
        out_shape=jax.ShapeDtypeStruct((cap, d), src.dtype),
        scratch_shapes=[pltpu.SemaphoreType.DMA((1,))],
        compiler_params=_cparams(("arbitrary",)),
        name="moe_gather",
    )(idx3, src)


def _combine_kernel(d1_ref, d2_ref, w1_ref, w2_ref, h_ref, g_ref, y_hbm, o_ref, y1_ref, y2_ref, sem):
    _gather_rows(d1_ref, y_hbm, y1_ref, sem.at[0])
    _gather_rows(d2_ref, y_hbm, y2_ref, sem.at[1])
    f = w1_ref[...] * y1_ref[...] + w2_ref[...] * y2_ref[...]
    o_ref[...] = h_ref[...] + _rms(f, g_ref[...])


def _combine(dest1, dest2, w1, w2, h2d, g, y, tc):
    n, d = h2d.shape
    idx = lambda v: v.reshape(n // tc, 1, tc)
    smem = pl.BlockSpec((1, 1, tc), lambda i: (i, 0, 0), memory_space=pltpu.SMEM)
    return pl.pallas_call(
        _combine_kernel,
        grid=(n // tc,),
        in_specs=[smem, smem,
                  pl.BlockSpec((tc, 1), lambda i: (i, 0)), pl.BlockSpec((tc, 1), lambda i: (i, 0)),
                  pl.BlockSpec((tc, d), lambda i: (i, 0)), pl.BlockSpec((1, d), lambda i: (0, 0)),
                  pl.BlockSpec(memory_space=pl.ANY)],
        out_specs=pl.BlockSpec((tc, d), lambda i: (i, 0)),
        out_shape=jax.ShapeDtypeStruct((n, d), F32),
        scratch_shapes=[pltpu.VMEM((tc, d), y.dtype), pltpu.VMEM((tc, d), y.dtype),
                        pltpu.SemaphoreType.DMA((2,))],
        compiler_params=_cparams(("arbitrary",)),
        name="moe_combine",
    )(idx(dest1), idx(dest2), w1, w2, h2d, g, y)


def _moe(hn_f32, h2d, w_router, wg, wu, wd, g_post, tiles):
    n, d = hn_f32.shape
    tb = tiles["moe_rows"]
    oi, ow, cnt = _route(hn_f32, w_router.T, tiles["route"])
    e1, e2, r1, r2 = oi[0], oi[1], oi[2], oi[3]
    counts = cnt[:, 0]
    padded = ((counts + tb - 1) // tb) * tb
    pad_end = jnp.cumsum(padded)
    pad_start = pad_end - padded
    dest1 = pad_start[e1] + r1
    dest2 = pad_start[e2] + r2
    n_blocks = -(-(2 * n) // tb) + N_EXPERTS
    tok = jnp.arange(n, dtype=jnp.int32)
    slot_tok = jnp.zeros((n_blocks * tb,), jnp.int32).at[jnp.concatenate([dest1, dest2])].set(
        jnp.concatenate([tok, tok]))
    block_start = jnp.arange(n_blocks, dtype=jnp.int32) * tb
    block_expert = jnp.minimum(jnp.searchsorted(pad_end, block_start, side="right"),
                               N_EXPERTS - 1).astype(jnp.int32)
    xs = _gather(slot_tok, hn_f32, tiles["gather"])
    y = _ffn_moe(block_expert, xs, wg, wu, wd, tb, tiles["ff"])
    return _combine(dest1, dest2, ow[0].reshape(n, 1), ow[1].reshape(n, 1), h2d, g_post, y,
                    tiles["combine"])


def _rope_tables(seq):
    t = jnp.arange(seq, dtype=jnp.int32)

    def cs(pos, dim):
        inv = ROPE_THETA ** (-jnp.arange(0, dim, 2, dtype=F32) / dim)
        ang = pos.astype(F32)[:, None] * inv[None, :]
        return jnp.cos(ang), jnp.sin(ang)

    rc, rs = cs(t // GRID_W, HEAD_DIM // 2)
    cc, cs_ = cs(t % GRID_W, HEAD_DIM // 2)
    sc, ss = cs(t, HEAD_DIM)
    cos_a = jnp.concatenate([rc, rc, cc, cc] * 2, axis=1)
    sin_a = jnp.concatenate([-rs, rs, -cs_, cs_] * 2, axis=1)
    cos_b = jnp.concatenate([sc, sc] * 2, axis=1)
    sin_b = jnp.concatenate([-ss, ss] * 2, axis=1)
    return cos_a, sin_a, cos_b, sin_b


def _tiles(n, seq, d_ff):
    pick = lambda total, want: math.gcd(total, want)
    return {
        "rows": pick(seq, 512),
        "tq_a": pick(seq, 128), "tq_b": pick(seq, 256), "tk": pick(seq, 512),
        "ffn_rows": pick(n, 1024), "ff": pick(d_ff, 512),
        "moe_rows": 1024 if n >= 4096 else 128, "route": pick(n, 512),
        "gather": pick(n, 512) if n >= 4096 else 128, "combine": pick(n, 256),
    }


def kernel(x, w_in, w_out, g_pre_mix, g_post_mix, g_pre_ffn, g_post_ffn, g_qnorm_a, g_knorm_a, g_out_a, diff_lambda, g_subln_b, w_gate_dense, w_up_dense, w_down_dense, w_router, w_gate_moe, w_up_moe, w_down_moe):
    b, s, d = x.shape
    n = b * s
    depth = w_in.shape[0]
    tiles = _tiles(n, s, w_gate_dense.shape[-1])
    tabs = _rope_tables(s)
    bd = (lax.broadcasted_iota(jnp.int32, (A_Q, A_Q), 0) // HEAD_DIM
          == lax.broadcasted_iota(jnp.int32, (A_Q, A_Q), 1) // HEAD_DIM).astype(BF16)
    row = lambda v: v.reshape(1, -1).astype(F32)

    h = x.reshape(n, d)
    for i in range(depth):
        gq = row(jnp.tile(g_qnorm_a[i], N_HEADS_A)) * Q_SCALE
        gk = row(jnp.tile(g_knorm_a[i], N_KV_A))
        qa, ka, va, qb, kb, vb = _inproj(h, row(g_pre_mix[i]), w_in[i].astype(BF16), gq, gk, tabs, bd,
                                         s, tiles["rows"])
        shp = lambda v: v.reshape(b, s, v.shape[-1])
        oa = _attn_a(shp(qa), shp(ka), shp(va), tiles["tq_a"], tiles["tk"])
        lam_init = 0.8 - 0.6 * math.exp(-0.3 * i)
        ob = _attn_b(diff_lambda[i].astype(F32), row(g_subln_b[i]), shp(qb), shp(kb), shp(vb),
                     tiles["tq_b"], tiles["tk"], lam_init)
        moe_layer = i % 2 == 1
        h, hn = _outproj(oa.reshape(n, A_Q), ob.reshape(n, B_V), h, w_out[i].astype(BF16),
                         row(g_out_a[i]), row(g_post_mix[i]), row(g_pre_ffn[i]),
                         F32 if moe_layer else BF16, tiles["rows"])
        j = i // 2
        if moe_layer:
            h = _moe(hn, h, w_router[j], w_gate_moe[j].astype(BF16), w_up_moe[j].astype(BF16),
                     w_down_moe[j].astype(BF16), row(g_post_ffn[i]), tiles)
        else:
            h = _ffn_dense(hn, w_gate_dense[j].astype(BF16), w_up_dense[j].astype(BF16),
                           w_down_dense[j].astype(BF16), h, row(g_post_ffn[i]),
                           tiles["ffn_rows"], tiles["ff"])
    return h.reshape(b, s, d)
```

```python
import functools
import math

import jax
import jax.numpy as jnp
from jax import lax
from jax.experimental import pallas as pl
from jax.experimental.pallas import tpu as pltpu

F32 = jnp.float32
BF16 = jnp.bfloat16

HEAD_DIM = 64
N_HEADS_A = 8
N_KV_A = 2
N_HEADS_B = 4
A_Q = N_HEADS_A * HEAD_DIM
A_KV = N_KV_A * HEAD_DIM
B_QK = N_HEADS_B * 2 * HEAD_DIM
B_V = N_HEADS_B * 2 * HEAD_DIM
GRID_W = 64
ROPE_THETA = 10000.0
EPS = 1e-6
N_EXPERTS = 8
LOG2E = 1.4426950408889634
LANES = 128
Q_SCALE = (HEAD_DIM ** -0.5) * LOG2E

VMEM_LIMIT = 56 * 1024 * 1024
ROW_UNROLL = 8


def _cparams(sem):
    return pltpu.CompilerParams(dimension_semantics=sem, vmem_limit_bytes=VMEM_LIMIT)


def _rms(x, g):
    return x * lax.rsqrt(jnp.mean(x * x, axis=-1, keepdims=True) + EPS) * g


def _lane(shape):
    return lax.broadcasted_iota(jnp.int32, shape, len(shape) - 1)


def _tile4(t):
    return jnp.concatenate([t, t, t, t], axis=1)


def _store_row_tiles(ref, x):
    for c in range(ref.shape[1]):
        ref[:, c, :] = x[:, c * LANES:(c + 1) * LANES]


def _load_row_tiles(ref):
    return jnp.concatenate([ref[:, c, :] for c in range(ref.shape[1])], axis=1)


def _rotate_pairs(x, half):
    w = x.shape[-1]
    first = (_lane(x.shape) % (2 * half)) < half
    return jnp.where(first, pltpu.roll(x, w - half, axis=1), pltpu.roll(x, half, axis=1))


def _inproj_kernel(h_ref, g_ref, w_ref, gq_ref, gk_ref, ca_ref, sa_ref, cb_ref, sb_ref, bd_ref,
                   qa_ref, ka_ref, va_ref, qb_ref, kb_ref, vb_ref):
    hn = _rms(h_ref[...], g_ref[...]).astype(BF16)
    tm = hn.shape[0]
    lane = _lane((tm, LANES))
    low = lane < HEAD_DIM
    ca, sa = ca_ref[...], sa_ref[...]
    cb, sb = cb_ref[...], sb_ref[...]

    def proj(lo, hi):
        return jnp.dot(hn, w_ref[:, lo:hi], preferred_element_type=F32)

    def head_norm(x, bd, g):
        ms = jnp.dot((x * x).astype(BF16), bd, preferred_element_type=F32) * (1.0 / HEAD_DIM)
        return x * lax.rsqrt(ms + EPS) * g

    qa = head_norm(proj(0, A_Q), bd_ref[...], gq_ref[...])
    qa = qa * _tile4(ca) + _rotate_pairs(qa, HEAD_DIM // 4) * _tile4(sa)
    for i in range(N_HEADS_A // 2):
        slab = qa[:, i * LANES:(i + 1) * LANES]
        qa_ref[:, (2 * i) * LANES:(2 * i + 1) * LANES] = jnp.where(low, slab, 0.0).astype(BF16)
        qa_ref[:, (2 * i + 1) * LANES:(2 * i + 2) * LANES] = jnp.where(
            low, pltpu.roll(slab, HEAD_DIM, axis=1), 0.0).astype(BF16)

    ka = head_norm(proj(A_Q, A_Q + A_KV), bd_ref[0:LANES, 0:LANES], gk_ref[...])
    ka = ka * ca + _rotate_pairs(ka, HEAD_DIM // 4) * sa
    ka_ref[:, 0:LANES] = jnp.where(low, ka, 0.0).astype(BF16)
    ka_ref[:, LANES:2 * LANES] = jnp.where(low, pltpu.roll(ka, HEAD_DIM, axis=1), 0.0).astype(BF16)

    va = proj(A_Q + A_KV, A_Q + 2 * A_KV)
    one_col = jnp.where(lane == HEAD_DIM, 1.0, 0.0)
    va_ref[:, 0:LANES] = jnp.where(low, va, one_col).astype(BF16)
    va_ref[:, LANES:2 * LANES] = jnp.where(low, pltpu.roll(va, HEAD_DIM, axis=1), one_col).astype(BF16)

    o = A_Q + 2 * A_KV
    qb = proj(o, o + B_QK)
    qb = (qb * _tile4(cb) + _rotate_pairs(qb, HEAD_DIM // 2) * _tile4(sb)) * Q_SCALE
    qb_ref[...] = qb.astype(BF16)
    kb = proj(o + B_QK, o + 2 * B_QK)
    kb = kb * _tile4(cb) + _rotate_pairs(kb, HEAD_DIM // 2) * _tile4(sb)
    kb_ref[...] = kb.astype(BF16)
    vb = proj(o + 2 * B_QK, o + 2 * B_QK + B_V)
    first_col = jnp.where(lane == 0, 1.0, 0.0).astype(BF16)
    for hd in range(N_HEADS_B):
        vb_ref[:, (2 * hd) * LANES:(2 * hd + 1) * LANES] = vb[:, hd * LANES:(hd + 1) * LANES].astype(BF16)
        vb_ref[:, (2 * hd + 1) * LANES:(2 * hd + 2) * LANES] = first_col


def _inproj(h2d, g, w_bf, gq, gk, tabs, bd, seq, tm):
    n, d = h2d.shape
    nt = seq // tm
    ca, sa, cb, sb = tabs
    row = lambda i: (i, 0)
    const = lambda i: (0, 0)
    tab = lambda i: (i % nt, 0)
    widths = (N_HEADS_A * LANES, N_KV_A * LANES, N_KV_A * LANES, B_QK, B_QK, 2 * B_V)
    return pl.pallas_call(
        _inproj_kernel,
        grid=(n // tm,),
        in_specs=[pl.BlockSpec((tm, d), row), pl.BlockSpec((1, d), const),
                  pl.BlockSpec(w_bf.shape, const), pl.BlockSpec((1, A_Q), const),
                  pl.BlockSpec((1, A_KV), const)]
                 + [pl.BlockSpec((tm, LANES), tab)] * 4
                 + [pl.BlockSpec(bd.shape, const)],
        out_specs=[pl.BlockSpec((tm, w), row) for w in widths],
        out_shape=[jax.ShapeDtypeStruct((n, w), BF16) for w in widths],
        compiler_params=_cparams(("parallel",)),
        name="inproj",
    )(h2d, g, w_bf, gq, gk, ca, sa, cb, sb, bd)


def _flash(q, k_ref, v_ref, tk):
    m_rows = q.shape[0]
    nk = k_ref.shape[0] // tk

    def scores(c):
        return lax.dot_general(q, k_ref[c * tk:(c + 1) * tk, :], (((1,), (1,)), ((), ())),
                               preferred_element_type=F32)

    m = jnp.full((m_rows, 1), -jnp.inf, F32)
    acc = jnp.zeros((m_rows, v_ref.shape[1]), F32)
    s = scores(0)
    for c in range(nk):
        s_next = scores(c + 1) if c + 1 < nk else None
        m_new = jnp.maximum(m, jnp.max(s, axis=-1, keepdims=True))
        p = jnp.exp2((s - m_new).astype(BF16))
        acc = jnp.exp2(m - m_new) * acc + jnp.dot(p, v_ref[c * tk:(c + 1) * tk, :],
                                                  preferred_element_type=F32)
        m, s = m_new, s_next
    return acc


def _attn_a_kernel(q_ref, k_ref, v_ref, o_ref, *, tk):
    tq = q_ref.shape[0]
    g = N_HEADS_A // N_KV_A
    q = jnp.concatenate([q_ref[:, j * LANES:(j + 1) * LANES] for j in range(g)], axis=0)
    acc = _flash(q, k_ref, v_ref, tk)
    o = acc * (1.0 / acc[:, HEAD_DIM:HEAD_DIM + 1])
    low = _lane((tq, LANES)) < HEAD_DIM
    slabs = [jnp.where(low, o[(2 * i) * tq:(2 * i + 1) * tq],
                       pltpu.roll(o[(2 * i + 1) * tq:(2 * i + 2) * tq], HEAD_DIM, axis=1))
             for i in range(g // 2)]
    o_ref[...] = jnp.concatenate(slabs, axis=1).astype(o_ref.dtype)


def _attn_a(qa, ka, va, tq, tk):
    b, s, _ = qa.shape
    g = N_HEADS_A // N_KV_A
    return pl.pallas_call(
        functools.partial(_attn_a_kernel, tk=tk),
        grid=(b, N_KV_A, s // tq),
        in_specs=[pl.BlockSpec((None, tq, g * LANES), lambda bi, h, i: (bi, i, h)),
                  pl.BlockSpec((None, s, LANES), lambda bi, h, i: (bi, 0, h)),
                  pl.BlockSpec((None, s, LANES), lambda bi, h, i: (bi, 0, h))],
        out_specs=pl.BlockSpec((None, tq, g * HEAD_DIM), lambda bi, h, i: (bi, i, h)),
        out_shape=jax.ShapeDtypeStruct((b, s, A_Q), BF16),
        compiler_params=_cparams(("parallel", "parallel", "parallel")),
        name="attn_a",
    )(qa, ka, va)


def _attn_b_kernel(lam_ref, gs_ref, q_ref, k_ref, v_ref, o_ref, *, tk, lam_init):
    tq = q_ref.shape[0]
    q = q_ref[...]
    low = _lane((tq, LANES)) < HEAD_DIM
    zero = jnp.zeros_like(q)
    qs = jnp.concatenate([jnp.where(low, q, zero), jnp.where(low, zero, q)], axis=0)
    acc = _flash(qs, k_ref, v_ref, tk)
    w = 2 * HEAD_DIM
    o = acc[:, 0:w] * (1.0 / acc[:, w:w + 1])
    lp = lam_ref[...]
    lam = (jnp.exp(jnp.sum(lp[0:1] * lp[1:2], axis=-1, keepdims=True))
           - jnp.exp(jnp.sum(lp[2:3] * lp[3:4], axis=-1, keepdims=True)) + lam_init)
    d = o[0:tq] - lam * o[tq:2 * tq]
    o_ref[...] = (_rms(d, gs_ref[...]) * (1.0 - lam_init)).astype(o_ref.dtype)


def _attn_b(lam_p, gs, qb, kb, vb, tq, tk, lam_init):
    b, s, _ = qb.shape
    return pl.pallas_call(
        functools.partial(_attn_b_kernel, tk=tk, lam_init=lam_init),
        grid=(b, N_HEADS_B, s // tq),
        in_specs=[pl.BlockSpec(lam_p.shape, lambda bi, h, i: (0, 0)),
                  pl.BlockSpec(gs.shape, lambda bi, h, i: (0, 0)),
                  pl.BlockSpec((None, tq, LANES), lambda bi, h, i: (bi, i, h)),
                  pl.BlockSpec((None, s, LANES), lambda bi, h, i: (bi, 0, h)),
                  pl.BlockSpec((None, s, 2 * LANES), lambda bi, h, i: (bi, 0, h))],
        out_specs=pl.BlockSpec((None, tq, LANES), lambda bi, h, i: (bi, i, h)),
        out_shape=jax.ShapeDtypeStruct((b, s, B_V), BF16),
        compiler_params=_cparams(("parallel", "parallel", "parallel")),
        name="attn_b",
    )(lam_p, gs, qb, kb, vb)


def _outproj_kernel(oa_ref, ob_ref, h_ref, w_ref, ga_ref, gpost_ref, gpre_ref, h2_ref, hn_ref):
    a = _rms(oa_ref[...].astype(F32), ga_ref[...]).astype(BF16)
    mix = (jnp.dot(a, w_ref[0:A_Q, :], preferred_element_type=F32)
           + jnp.dot(ob_ref[...], w_ref[A_Q:, :], preferred_element_type=F32))
    h2 = h_ref[...] + _rms(mix, gpost_ref[...])
    h2_ref[...] = h2
    hn = _rms(h2, gpre_ref[...])
    if len(hn_ref.shape) == 3:
        _store_row_tiles(hn_ref, hn)
    else:
        hn_ref[...] = hn.astype(hn_ref.dtype)


def _outproj(oa, ob, h2d, w_bf, ga, gpost, gpre, row_tiles, tm):
    n, d = h2d.shape
    row = lambda i: (i, 0)
    const = lambda i: (0, 0)
    if row_tiles:
        hn_spec = pl.BlockSpec((tm, d // LANES, LANES), lambda i: (i, 0, 0))
        hn_shape = jax.ShapeDtypeStruct((n, d // LANES, LANES), F32)
    else:
        hn_spec = pl.BlockSpec((tm, d), row)
        hn_shape = jax.ShapeDtypeStruct((n, d), BF16)
    return pl.pallas_call(
        _outproj_kernel,
        grid=(n // tm,),
        in_specs=[pl.BlockSpec((tm, A_Q), row), pl.BlockSpec((tm, B_V), row),
                  pl.BlockSpec((tm, d), row), pl.BlockSpec(w_bf.shape, const),
                  pl.BlockSpec((1, A_Q), const), pl.BlockSpec((1, d), const),
                  pl.BlockSpec((1, d), const)],
        out_specs=[pl.BlockSpec((tm, d), row), hn_spec],
        out_shape=[jax.ShapeDtypeStruct((n, d), F32), hn_shape],
        compiler_params=_cparams(("parallel",)),
        name="outproj",
    )(oa, ob, h2d, w_bf, ga, gpost, gpre)


def _swiglu_step(x, wg_ref, wu_ref, wd_ref, acc_ref):
    j = pl.program_id(1)

    @pl.when(j == 0)
    def _():
        acc_ref[...] = jnp.zeros_like(acc_ref)

    gate = jnp.dot(x, wg_ref[...], preferred_element_type=F32)
    up = jnp.dot(x, wu_ref[...], preferred_element_type=F32)
    act = (gate * jax.nn.sigmoid(gate) * up).astype(BF16)
    acc_ref[...] += jnp.dot(act, wd_ref[...], preferred_element_type=F32)


def _ffn_dense_kernel(x_ref, wg_ref, wu_ref, wd_ref, h_ref, g_ref, o_ref, acc_ref):
    _swiglu_step(x_ref[...], wg_ref, wu_ref, wd_ref, acc_ref)

    @pl.when(pl.program_id(1) == pl.num_programs(1) - 1)
    def _():
        o_ref[...] = h_ref[...] + _rms(acc_ref[...], g_ref[...])


def _ffn_dense(x_bf, wg, wu, wd, h2d, g, tm, tf):
    n, d = h2d.shape
    f = wg.shape[1]
    return pl.pallas_call(
        _ffn_dense_kernel,
        grid=(n // tm, f // tf),
        in_specs=[pl.BlockSpec((tm, d), lambda i, j: (i, 0)),
                  pl.BlockSpec((d, tf), lambda i, j: (0, j)),
                  pl.BlockSpec((d, tf), lambda i, j: (0, j)),
                  pl.BlockSpec((tf, d), lambda i, j: (j, 0)),
                  pl.BlockSpec((tm, d), lambda i, j: (i, 0)),
                  pl.BlockSpec((1, d), lambda i, j: (0, 0))],
        out_specs=pl.BlockSpec((tm, d), lambda i, j: (i, 0)),
        out_shape=jax.ShapeDtypeStruct((n, d), F32),
        scratch_shapes=[pltpu.VMEM((tm, d), F32)],
        compiler_params=_cparams(("parallel", "arbitrary")),
        name="ffn_dense",
    )(x_bf, wg, wu, wd, h2d, g)


def _ffn_moe_kernel(be_ref, x_ref, wg_ref, wu_ref, wd_ref, o_ref, acc_ref):
    del be_ref
    _swiglu_step(x_ref[...].astype(BF16), wg_ref, wu_ref, wd_ref, acc_ref)

    @pl.when(pl.program_id(1) == pl.num_programs(1) - 1)
    def _():
        _store_row_tiles(o_ref, acc_ref[...])


def _ffn_moe(block_expert, xs, wg, wu, wd, tb, tf):
    cap, d = xs.shape
    f = wg.shape[2]
    grid_spec = pltpu.PrefetchScalarGridSpec(
        num_scalar_prefetch=1,
        grid=(cap // tb, f // tf),
        in_specs=[pl.BlockSpec((tb, d), lambda i, j, be: (i, 0)),
                  pl.BlockSpec((None, d, tf), lambda i, j, be: (be[i], 0, j)),
                  pl.BlockSpec((None, d, tf), lambda i, j, be: (be[i], 0, j)),
                  pl.BlockSpec((None, tf, d), lambda i, j, be: (be[i], j, 0))],
        out_specs=pl.BlockSpec((tb, d // LANES, LANES), lambda i, j, be: (i, 0, 0)),
        scratch_shapes=[pltpu.VMEM((tb, d), F32)],
    )
    return pl.pallas_call(
        _ffn_moe_kernel,
        grid_spec=grid_spec,
        out_shape=jax.ShapeDtypeStruct((cap, d // LANES, LANES), F32),
        compiler_params=_cparams(("parallel", "arbitrary")),
        name="ffn_moe",
    )(block_expert, xs, wg, wu, wd)


def _route_kernel(x_ref, wr_ref, tri_ref, oi_ref, ow_ref, cnt_ref, carry_ref):
    @pl.when(pl.program_id(0) == 0)
    def _():
        carry_ref[...] = jnp.zeros_like(carry_ref)

    logits = lax.dot_general(wr_ref[...], _load_row_tiles(x_ref), (((1,), (1,)), ((), ())),
                             precision=lax.Precision.HIGHEST, preferred_element_type=F32)
    shape = logits.shape
    eidx = lax.broadcasted_iota(jnp.int32, shape, 0)
    m1 = jnp.max(logits, axis=0, keepdims=True)
    i1 = jnp.min(jnp.where(logits == m1, eidx, N_EXPERTS), axis=0, keepdims=True)
    oh1 = eidx == i1
    rest = jnp.where(oh1, -jnp.inf, logits)
    m2 = jnp.max(rest, axis=0, keepdims=True)
    i2 = jnp.min(jnp.where(rest == m2, eidx, N_EXPERTS), axis=0, keepdims=True)
    oh2 = eidx == i2
    e = jnp.exp(m2 - m1)
    w1 = 1.0 / (1.0 + e)
    w2 = e * w1

    cnt = jnp.where(oh1, 1.0, jnp.where(oh2, 1.0, 0.0))
    incl = jnp.dot(cnt.astype(BF16), tri_ref[...], preferred_element_type=F32)
    excl = incl - cnt + carry_ref[:, 0:1]
    r1 = jnp.sum(jnp.where(oh1, excl, 0.0), axis=0, keepdims=True).astype(jnp.int32)
    r2 = jnp.sum(jnp.where(oh2, excl, 0.0), axis=0, keepdims=True).astype(jnp.int32)
    carry_ref[...] = carry_ref[...] + jnp.sum(cnt, axis=1, keepdims=True)

    bc = lambda v: jnp.broadcast_to(v, shape)
    oi_ref[...] = jnp.where(eidx == 0, bc(i1), jnp.where(eidx == 1, bc(i2),
                            jnp.where(eidx == 2, bc(r1), bc(r2))))
    ow_ref[...] = jnp.where(eidx == 0, bc(w1), bc(w2))
    cnt_ref[...] = carry_ref[...].astype(jnp.int32)


def _route(hn_tiles, wr_t, tr):
    n = hn_tiles.shape[0]
    tri = (lax.broadcasted_iota(jnp.int32, (tr, tr), 0)
           <= lax.broadcasted_iota(jnp.int32, (tr, tr), 1)).astype(BF16)
    return pl.pallas_call(
        _route_kernel,
        grid=(n // tr,),
        in_specs=[pl.BlockSpec((tr,) + hn_tiles.shape[1:], lambda i: (i, 0, 0)),
                  pl.BlockSpec(wr_t.shape, lambda i: (0, 0)),
                  pl.BlockSpec((tr, tr), lambda i: (0, 0))],
        out_specs=[pl.BlockSpec((N_EXPERTS, tr), lambda i: (0, i)),
                   pl.BlockSpec((N_EXPERTS, tr), lambda i: (0, i)),
                   pl.BlockSpec((N_EXPERTS, LANES), lambda i: (0, 0))],
        out_shape=[jax.ShapeDtypeStruct((N_EXPERTS, n), jnp.int32),
                   jax.ShapeDtypeStruct((N_EXPERTS, n), F32),
                   jax.ShapeDtypeStruct((N_EXPERTS, LANES), jnp.int32)],
        scratch_shapes=[pltpu.VMEM((N_EXPERTS, LANES), F32)],
        compiler_params=_cparams(("arbitrary",)),
        name="route",
    )(hn_tiles, wr_t, tri)


def _row_copy(src_hbm, dst_ref, sem, src_row, dst_row):
    return pltpu.make_async_copy(src_hbm.at[pl.ds(src_row, 1)], dst_ref.at[pl.ds(dst_row, 1)], sem)


def _start_rows(idx_ref, src_hbm, dst_ref, sem):
    rows = dst_ref.shape[0]

    def group(g, c):
        base = g * ROW_UNROLL
        for u in range(ROW_UNROLL):
            _row_copy(src_hbm, dst_ref, sem, idx_ref[0, 0, base + u], base + u).start(priority=u % 2)
        return c

    lax.fori_loop(0, rows // ROW_UNROLL, group, 0)


def _wait_rows(src_hbm, dst_ref, sem):
    for r in range(dst_ref.shape[0]):
        _row_copy(src_hbm, dst_ref, sem, 0, r).wait()


def _ring_step(start, wait):
    i = pl.program_id(0)
    slot = i % 2

    @pl.when(i == 0)
    def _():
        start(0, False)

    @pl.when(i + 1 < pl.num_programs(0))
    def _():
        start(1 - slot, True)

    wait(slot)
    return slot


def _gather_kernel(idx_ref, idx_next_ref, src_hbm, o_ref, buf_ref, sem):
    def start(slot, nxt):
        _start_rows(idx_next_ref if nxt else idx_ref, src_hbm, buf_ref.at[slot], sem.at[slot])

    def wait(slot):
        _wait_rows(src_hbm, buf_ref.at[slot], sem.at[slot])

    slot = _ring_step(start, wait)
    o_ref[...] = _load_row_tiles(buf_ref.at[slot])


def _next_block(nb):
    return lambda i: (jnp.minimum(i + 1, nb - 1), 0, 0)


def _gather(slot_tok, src, rows):
    cap = slot_tok.shape[0]
    tile = src.shape[1:]
    d = tile[0] * tile[1]
    nb = cap // rows
    idx3 = slot_tok.reshape(nb, 1, rows)
    return pl.pallas_call(
        _gather_kernel,
        grid=(nb,),
        in_specs=[pl.BlockSpec((1, 1, rows), lambda i: (i, 0, 0), memory_space=pltpu.SMEM),
                  pl.BlockSpec((1, 1, rows), _next_block(nb), memory_space=pltpu.SMEM),
                  pl.BlockSpec(memory_space=pl.ANY)],
        out_specs=pl.BlockSpec((rows, d), lambda i: (i, 0)),
        out_shape=jax.ShapeDtypeStruct((cap, d), src.dtype),
        scratch_shapes=[pltpu.VMEM((2, rows) + tile, src.dtype), pltpu.SemaphoreType.DMA((2,))],
        compiler_params=_cparams(("arbitrary",)),
        name="moe_gather",
    )(idx3, idx3, src)


def _combine_kernel(d1_ref, d2_ref, d1n_ref, d2n_ref, w1_ref, w2_ref, h_ref, g_ref, y_hbm, o_ref,
                    y1_ref, y2_ref, sem):
    def start(slot, nxt):
        _start_rows(d1n_ref if nxt else d1_ref, y_hbm, y1_ref.at[slot], sem.at[0, slot])
        _start_rows(d2n_ref if nxt else d2_ref, y_hbm, y2_ref.at[slot], sem.at[1, slot])

    def wait(slot):
        _wait_rows(y_hbm, y1_ref.at[slot], sem.at[0, slot])
        _wait_rows(y_hbm, y2_ref.at[slot], sem.at[1, slot])

    slot = _ring_step(start, wait)
    f = (w1_ref[...] * _load_row_tiles(y1_ref.at[slot])
         + w2_ref[...] * _load_row_tiles(y2_ref.at[slot]))
    o_ref[...] = h_ref[...] + _rms(f, g_ref[...])


def _combine(dest1, dest2, w1, w2, h2d, g, y, tc):
    n, d = h2d.shape
    nb = n // tc
    idx = lambda v: v.reshape(nb, 1, tc)
    smem = pl.BlockSpec((1, 1, tc), lambda i: (i, 0, 0), memory_space=pltpu.SMEM)
    smem_next = pl.BlockSpec((1, 1, tc), _next_block(nb), memory_space=pltpu.SMEM)
    return pl.pallas_call(
        _combine_kernel,
        grid=(nb,),
        in_specs=[smem, smem, smem_next, smem_next,
                  pl.BlockSpec((tc, 1), lambda i: (i, 0)), pl.BlockSpec((tc, 1), lambda i: (i, 0)),
                  pl.BlockSpec((tc, d), lambda i: (i, 0)), pl.BlockSpec((1, d), lambda i: (0, 0)),
                  pl.BlockSpec(memory_space=pl.ANY)],
        out_specs=pl.BlockSpec((tc, d), lambda i: (i, 0)),
        out_shape=jax.ShapeDtypeStruct((n, d), F32),
        scratch_shapes=[pltpu.VMEM((2, tc) + y.shape[1:], y.dtype),
                        pltpu.VMEM((2, tc) + y.shape[1:], y.dtype),
                        pltpu.SemaphoreType.DMA((2, 2))],
        compiler_params=_cparams(("arbitrary",)),
        name="moe_combine",
    )(idx(dest1), idx(dest2), idx(dest1), idx(dest2), w1, w2, h2d, g, y)


def _moe(hn_tiles, h2d, w_router, wg, wu, wd, g_post, tiles):
    n = hn_tiles.shape[0]
    tb = tiles["moe_rows"]
    oi, ow, cnt = _route(hn_tiles, w_router.T, tiles["route"])
    e1, e2, r1, r2 = oi[0], oi[1], oi[2], oi[3]
    counts = cnt[:, 0]
    padded = ((counts + tb - 1) // tb) * tb
    pad_end = jnp.cumsum(padded)
    pad_start = pad_end - padded
    dest1 = pad_start[e1] + r1
    dest2 = pad_start[e2] + r2
    n_blocks = -(-(2 * n) // tb) + N_EXPERTS
    tok = jnp.arange(n, dtype=jnp.int32)
    slot_tok = jnp.zeros((n_blocks * tb,), jnp.int32).at[jnp.concatenate([dest1, dest2])].set(
        jnp.concatenate([tok, tok]))
    block_start = jnp.arange(n_blocks, dtype=jnp.int32) * tb
    block_expert = jnp.minimum(jnp.searchsorted(pad_end, block_start, side="right"),
                               N_EXPERTS - 1).astype(jnp.int32)
    xs = _gather(slot_tok, hn_tiles, tiles["gather"])
    y = _ffn_moe(block_expert, xs, wg, wu, wd, tb, tiles["ff"])
    return _combine(dest1, dest2, ow[0].reshape(n, 1), ow[1].reshape(n, 1), h2d, g_post, y,
                    tiles["combine"])


def _rope_tables(seq):
    t = jnp.arange(seq, dtype=jnp.int32)

    def cs(pos, dim):
        inv = ROPE_THETA ** (-jnp.arange(0, dim, 2, dtype=F32) / dim)
        ang = pos.astype(F32)[:, None] * inv[None, :]
        return jnp.cos(ang), jnp.sin(ang)

    rc, rs = cs(t // GRID_W, HEAD_DIM // 2)
    cc, cs_ = cs(t % GRID_W, HEAD_DIM // 2)
    sc, ss = cs(t, HEAD_DIM)
    cos_a = jnp.concatenate([rc, rc, cc, cc] * 2, axis=1)
    sin_a = jnp.concatenate([-rs, rs, -cs_, cs_] * 2, axis=1)
    cos_b = jnp.concatenate([sc, sc] * 2, axis=1)
    sin_b = jnp.concatenate([-ss, ss] * 2, axis=1)
    return cos_a, sin_a, cos_b, sin_b


def _tiles(n, seq, d_ff):
    pick = lambda total, want: math.gcd(total, want)
    return {
        "rows": pick(seq, 512),
        "tq_a": pick(seq, 128), "tq_b": pick(seq, 256), "tk": pick(seq, 512),
        "ffn_rows": pick(n, 1024), "ff": pick(d_ff, 512),
        "moe_rows": 1024 if n >= 4096 else 128, "route": pick(n, 512),
        "gather": pick(n, 512) if n >= 4096 else 128, "combine": pick(n, 256),
    }


def kernel(x, w_in, w_out, g_pre_mix, g_post_mix, g_pre_ffn, g_post_ffn, g_qnorm_a, g_knorm_a, g_out_a, diff_lambda, g_subln_b, w_gate_dense, w_up_dense, w_down_dense, w_router, w_gate_moe, w_up_moe, w_down_moe):
    b, s, d = x.shape
    n = b * s
    depth = w_in.shape[0]
    tiles = _tiles(n, s, w_gate_dense.shape[-1])
    tabs = _rope_tables(s)
    bd = (lax.broadcasted_iota(jnp.int32, (A_Q, A_Q), 0) // HEAD_DIM
          == lax.broadcasted_iota(jnp.int32, (A_Q, A_Q), 1) // HEAD_DIM).astype(BF16)
    row = lambda v: v.reshape(1, -1).astype(F32)

    h = x.reshape(n, d)
    for i in range(depth):
        gq = row(jnp.tile(g_qnorm_a[i], N_HEADS_A)) * Q_SCALE
        gk = row(jnp.tile(g_knorm_a[i], N_KV_A))
        qa, ka, va, qb, kb, vb = _inproj(h, row(g_pre_mix[i]), w_in[i].astype(BF16), gq, gk, tabs, bd,
                                         s, tiles["rows"])
        shp = lambda v: v.reshape(b, s, v.shape[-1])
        oa = _attn_a(shp(qa), shp(ka), shp(va), tiles["tq_a"], tiles["tk"])
        lam_init = 0.8 - 0.6 * math.exp(-0.3 * i)
        ob = _attn_b(diff_lambda[i].astype(F32), row(g_subln_b[i]), shp(qb), shp(kb), shp(vb),
                     tiles["tq_b"], tiles["tk"], lam_init)
        moe_layer = i % 2 == 1
        h, hn = _outproj(oa.reshape(n, A_Q), ob.reshape(n, B_V), h, w_out[i].astype(BF16),
                         row(g_out_a[i]), row(g_post_mix[i]), row(g_pre_ffn[i]),
                         moe_layer, tiles["rows"])
        j = i // 2
        if moe_layer:
            h = _moe(hn, h, w_router[j], w_gate_moe[j].astype(BF16), w_up_moe[j].astype(BF16),
                     w_down_moe[j].astype(BF16), row(g_post_ffn[i]), tiles)
        else:
            h = _ffn_dense(hn, w_gate_dense[j].astype(BF16), w_up_dense[j].astype(BF16),
                           w_down_dense[j].astype(BF16), h, row(g_post_ffn[i]),
                           tiles["ffn_rows"], tiles["ff"])
    return h.reshape(b, s, d)
```

```python
import functools
import math

import jax
import jax.numpy as jnp
from jax import lax
from jax.experimental import pallas as pl
from jax.experimental.pallas import tpu as pltpu

F32 = jnp.float32
BF16 = jnp.bfloat16

HEAD_DIM = 64
N_HEADS_A = 8
N_KV_A = 2
N_HEADS_B = 4
A_Q = N_HEADS_A * HEAD_DIM
A_KV = N_KV_A * HEAD_DIM
B_QK = N_HEADS_B * 2 * HEAD_DIM
B_V = N_HEADS_B * 2 * HEAD_DIM
GRID_W = 64
ROPE_THETA = 10000.0
EPS = 1e-6
N_EXPERTS = 8
LOG2E = 1.4426950408889634
LANES = 128
SUBLANES = 8
D_MODEL = SUBLANES * LANES
Q_SCALE = (HEAD_DIM ** -0.5) * LOG2E

VMEM_LIMIT = 56 * 1024 * 1024
ROW_UNROLL = 8


def _cparams(sem):
    return pltpu.CompilerParams(dimension_semantics=sem, vmem_limit_bytes=VMEM_LIMIT)


def _rms(x, g):
    return x * lax.rsqrt(jnp.mean(x * x, axis=-1, keepdims=True) + EPS) * g


def _lane(shape):
    return lax.broadcasted_iota(jnp.int32, shape, len(shape) - 1)


def _tile4(t):
    return jnp.concatenate([t, t, t, t], axis=1)


def _store_row_tiles(ref, x):
    rows = x.shape[0]
    for c in range(SUBLANES):
        ref[pl.ds(c, rows, stride=SUBLANES), :] = x[:, c * LANES:(c + 1) * LANES]


def _load_row_tiles(ref):
    rows = ref.shape[0] // SUBLANES
    return jnp.concatenate([ref[pl.ds(c, rows, stride=SUBLANES), :] for c in range(SUBLANES)], axis=1)


def _rotate_pairs(x, half):
    w = x.shape[-1]
    first = (_lane(x.shape) % (2 * half)) < half
    return jnp.where(first, pltpu.roll(x, w - half, axis=1), pltpu.roll(x, half, axis=1))


def _inproj_kernel(h_ref, g_ref, w_ref, gq_ref, gk_ref, ca_ref, sa_ref, cb_ref, sb_ref, bd_ref,
                   qa_ref, ka_ref, va_ref, qb_ref, kb_ref, vb_ref):
    hn = _rms(h_ref[...], g_ref[...]).astype(BF16)
    tm = hn.shape[0]
    lane = _lane((tm, LANES))
    low = lane < HEAD_DIM
    ca, sa = ca_ref[...], sa_ref[...]
    cb, sb = cb_ref[...], sb_ref[...]

    def proj(lo, hi):
        return jnp.dot(hn, w_ref[:, lo:hi], preferred_element_type=F32)

    def head_norm(x, bd, g):
        ms = jnp.dot((x * x).astype(BF16), bd, preferred_element_type=F32) * (1.0 / HEAD_DIM)
        return x * lax.rsqrt(ms + EPS) * g

    qa = head_norm(proj(0, A_Q), bd_ref[...], gq_ref[...])
    qa = qa * _tile4(ca) + _rotate_pairs(qa, HEAD_DIM // 4) * _tile4(sa)
    for i in range(N_HEADS_A // 2):
        slab = qa[:, i * LANES:(i + 1) * LANES]
        qa_ref[:, (2 * i) * LANES:(2 * i + 1) * LANES] = jnp.where(low, slab, 0.0).astype(BF16)
        qa_ref[:, (2 * i + 1) * LANES:(2 * i + 2) * LANES] = jnp.where(
            low, pltpu.roll(slab, HEAD_DIM, axis=1), 0.0).astype(BF16)

    ka = head_norm(proj(A_Q, A_Q + A_KV), bd_ref[0:LANES, 0:LANES], gk_ref[...])
    ka = ka * ca + _rotate_pairs(ka, HEAD_DIM // 4) * sa
    ka_ref[:, 0:LANES] = jnp.where(low, ka, 0.0).astype(BF16)
    ka_ref[:, LANES:2 * LANES] = jnp.where(low, pltpu.roll(ka, HEAD_DIM, axis=1), 0.0).astype(BF16)

    va = proj(A_Q + A_KV, A_Q + 2 * A_KV)
    one_col = jnp.where(lane == HEAD_DIM, 1.0, 0.0)
    va_ref[:, 0:LANES] = jnp.where(low, va, one_col).astype(BF16)
    va_ref[:, LANES:2 * LANES] = jnp.where(low, pltpu.roll(va, HEAD_DIM, axis=1), one_col).astype(BF16)

    o = A_Q + 2 * A_KV
    qb = proj(o, o + B_QK)
    qb = (qb * _tile4(cb) + _rotate_pairs(qb, HEAD_DIM // 2) * _tile4(sb)) * Q_SCALE
    qb_ref[...] = qb.astype(BF16)
    kb = proj(o + B_QK, o + 2 * B_QK)
    kb = kb * _tile4(cb) + _rotate_pairs(kb, HEAD_DIM // 2) * _tile4(sb)
    kb_ref[...] = kb.astype(BF16)
    vb = proj(o + 2 * B_QK, o + 2 * B_QK + B_V)
    first_col = jnp.where(lane == 0, 1.0, 0.0).astype(BF16)
    for hd in range(N_HEADS_B):
        vb_ref[:, (2 * hd) * LANES:(2 * hd + 1) * LANES] = vb[:, hd * LANES:(hd + 1) * LANES].astype(BF16)
        vb_ref[:, (2 * hd + 1) * LANES:(2 * hd + 2) * LANES] = first_col


def _inproj(h2d, g, w_bf, gq, gk, tabs, bd, seq, tm):
    n, d = h2d.shape
    nt = seq // tm
    ca, sa, cb, sb = tabs
    row = lambda i: (i, 0)
    const = lambda i: (0, 0)
    tab = lambda i: (i % nt, 0)
    widths = (N_HEADS_A * LANES, N_KV_A * LANES, N_KV_A * LANES, B_QK, B_QK, 2 * B_V)
    return pl.pallas_call(
        _inproj_kernel,
        grid=(n // tm,),
        in_specs=[pl.BlockSpec((tm, d), row), pl.BlockSpec((1, d), const),
                  pl.BlockSpec(w_bf.shape, const), pl.BlockSpec((1, A_Q), const),
                  pl.BlockSpec((1, A_KV), const)]
                 + [pl.BlockSpec((tm, LANES), tab)] * 4
                 + [pl.BlockSpec(bd.shape, const)],
        out_specs=[pl.BlockSpec((tm, w), row) for w in widths],
        out_shape=[jax.ShapeDtypeStruct((n, w), BF16) for w in widths],
        compiler_params=_cparams(("parallel",)),
        name="inproj",
    )(h2d, g, w_bf, gq, gk, ca, sa, cb, sb, bd)


def _flash(q, k_ref, v_ref, tk):
    m_rows = q.shape[0]
    nk = k_ref.shape[0] // tk

    def scores(c):
        return lax.dot_general(q, k_ref[c * tk:(c + 1) * tk, :], (((1,), (1,)), ((), ())),
                               preferred_element_type=F32)

    m = jnp.full((m_rows, 1), -jnp.inf, F32)
    acc = jnp.zeros((m_rows, v_ref.shape[1]), F32)
    s = scores(0)
    for c in range(nk):
        s_next = scores(c + 1) if c + 1 < nk else None
        m_new = jnp.maximum(m, jnp.max(s, axis=-1, keepdims=True))
        p = jnp.exp2((s - m_new).astype(BF16))
        acc = jnp.exp2(m - m_new) * acc + jnp.dot(p, v_ref[c * tk:(c + 1) * tk, :],
                                                  preferred_element_type=F32)
        m, s = m_new, s_next
    return acc


def _attn_a_kernel(q_ref, k_ref, v_ref, o_ref, *, tk):
    tq = q_ref.shape[0]
    g = N_HEADS_A // N_KV_A
    q = jnp.concatenate([q_ref[:, j * LANES:(j + 1) * LANES] for j in range(g)], axis=0)
    acc = _flash(q, k_ref, v_ref, tk)
    o = acc * (1.0 / acc[:, HEAD_DIM:HEAD_DIM + 1])
    low = _lane((tq, LANES)) < HEAD_DIM
    slabs = [jnp.where(low, o[(2 * i) * tq:(2 * i + 1) * tq],
                       pltpu.roll(o[(2 * i + 1) * tq:(2 * i + 2) * tq], HEAD_DIM, axis=1))
             for i in range(g // 2)]
    o_ref[...] = jnp.concatenate(slabs, axis=1).astype(o_ref.dtype)


def _attn_a(qa, ka, va, tq, tk):
    b, s, _ = qa.shape
    g = N_HEADS_A // N_KV_A
    return pl.pallas_call(
        functools.partial(_attn_a_kernel, tk=tk),
        grid=(b, N_KV_A, s // tq),
        in_specs=[pl.BlockSpec((None, tq, g * LANES), lambda bi, h, i: (bi, i, h)),
                  pl.BlockSpec((None, s, LANES), lambda bi, h, i: (bi, 0, h)),
                  pl.BlockSpec((None, s, LANES), lambda bi, h, i: (bi, 0, h))],
        out_specs=pl.BlockSpec((None, tq, g * HEAD_DIM), lambda bi, h, i: (bi, i, h)),
        out_shape=jax.ShapeDtypeStruct((b, s, A_Q), BF16),
        compiler_params=_cparams(("parallel", "parallel", "parallel")),
        name="attn_a",
    )(qa, ka, va)


def _attn_b_kernel(lam_ref, gs_ref, q_ref, k_ref, v_ref, o_ref, *, tk, lam_init):
    tq = q_ref.shape[0]
    q = q_ref[...]
    low = _lane((tq, LANES)) < HEAD_DIM
    zero = jnp.zeros_like(q)
    qs = jnp.concatenate([jnp.where(low, q, zero), jnp.where(low, zero, q)], axis=0)
    acc = _flash(qs, k_ref, v_ref, tk)
    w = 2 * HEAD_DIM
    o = acc[:, 0:w] * (1.0 / acc[:, w:w + 1])
    lp = lam_ref[...]
    lam = (jnp.exp(jnp.sum(lp[0:1] * lp[1:2], axis=-1, keepdims=True))
           - jnp.exp(jnp.sum(lp[2:3] * lp[3:4], axis=-1, keepdims=True)) + lam_init)
    d = o[0:tq] - lam * o[tq:2 * tq]
    o_ref[...] = (_rms(d, gs_ref[...]) * (1.0 - lam_init)).astype(o_ref.dtype)


def _attn_b(lam_p, gs, qb, kb, vb, tq, tk, lam_init):
    b, s, _ = qb.shape
    return pl.pallas_call(
        functools.partial(_attn_b_kernel, tk=tk, lam_init=lam_init),
        grid=(b, N_HEADS_B, s // tq),
        in_specs=[pl.BlockSpec(lam_p.shape, lambda bi, h, i: (0, 0)),
                  pl.BlockSpec(gs.shape, lambda bi, h, i: (0, 0)),
                  pl.BlockSpec((None, tq, LANES), lambda bi, h, i: (bi, i, h)),
                  pl.BlockSpec((None, s, LANES), lambda bi, h, i: (bi, 0, h)),
                  pl.BlockSpec((None, s, 2 * LANES), lambda bi, h, i: (bi, 0, h))],
        out_specs=pl.BlockSpec((None, tq, LANES), lambda bi, h, i: (bi, i, h)),
        out_shape=jax.ShapeDtypeStruct((b, s, B_V), BF16),
        compiler_params=_cparams(("parallel", "parallel", "parallel")),
        name="attn_b",
    )(lam_p, gs, qb, kb, vb)


def _outproj_kernel(oa_ref, ob_ref, h_ref, w_ref, ga_ref, gpost_ref, gpre_ref, h2_ref, hn_ref):
    a = _rms(oa_ref[...].astype(F32), ga_ref[...]).astype(BF16)
    mix = (jnp.dot(a, w_ref[0:A_Q, :], preferred_element_type=F32)
           + jnp.dot(ob_ref[...], w_ref[A_Q:, :], preferred_element_type=F32))
    h2 = h_ref[...] + _rms(mix, gpost_ref[...])
    h2_ref[...] = h2
    hn = _rms(h2, gpre_ref[...])
    if hn_ref.shape[1] == LANES:
        _store_row_tiles(hn_ref, hn)
    else:
        hn_ref[...] = hn.astype(hn_ref.dtype)


def _outproj(oa, ob, h2d, w_bf, ga, gpost, gpre, row_tiles, tm):
    n, d = h2d.shape
    row = lambda i: (i, 0)
    const = lambda i: (0, 0)
    if row_tiles:
        hn_spec = pl.BlockSpec((tm * SUBLANES, LANES), row)
        hn_shape = jax.ShapeDtypeStruct((n * SUBLANES, LANES), F32)
    else:
        hn_spec = pl.BlockSpec((tm, d), row)
        hn_shape = jax.ShapeDtypeStruct((n, d), BF16)
    return pl.pallas_call(
        _outproj_kernel,
        grid=(n // tm,),
        in_specs=[pl.BlockSpec((tm, A_Q), row), pl.BlockSpec((tm, B_V), row),
                  pl.BlockSpec((tm, d), row), pl.BlockSpec(w_bf.shape, const),
                  pl.BlockSpec((1, A_Q), const), pl.BlockSpec((1, d), const),
                  pl.BlockSpec((1, d), const)],
        out_specs=[pl.BlockSpec((tm, d), row), hn_spec],
        out_shape=[jax.ShapeDtypeStruct((n, d), F32), hn_shape],
        compiler_params=_cparams(("parallel",)),
        name="outproj",
    )(oa, ob, h2d, w_bf, ga, gpost, gpre)


def _swiglu_step(x, wg_ref, wu_ref, wd_ref, acc_ref):
    j = pl.program_id(1)

    @pl.when(j == 0)
    def _():
        acc_ref[...] = jnp.zeros_like(acc_ref)

    gate = jnp.dot(x, wg_ref[...], preferred_element_type=F32)
    up = jnp.dot(x, wu_ref[...], preferred_element_type=F32)
    act = (gate * jax.nn.sigmoid(gate) * up).astype(BF16)
    acc_ref[...] += jnp.dot(act, wd_ref[...], preferred_element_type=F32)


def _ffn_dense_kernel(x_ref, wg_ref, wu_ref, wd_ref, h_ref, g_ref, o_ref, acc_ref):
    _swiglu_step(x_ref[...], wg_ref, wu_ref, wd_ref, acc_ref)

    @pl.when(pl.program_id(1) == pl.num_programs(1) - 1)
    def _():
        o_ref[...] = h_ref[...] + _rms(acc_ref[...], g_ref[...])


def _ffn_dense(x_bf, wg, wu, wd, h2d, g, tm, tf):
    n, d = h2d.shape
    f = wg.shape[1]
    return pl.pallas_call(
        _ffn_dense_kernel,
        grid=(n // tm, f // tf),
        in_specs=[pl.BlockSpec((tm, d), lambda i, j: (i, 0)),
                  pl.BlockSpec((d, tf), lambda i, j: (0, j)),
                  pl.BlockSpec((d, tf), lambda i, j: (0, j)),
                  pl.BlockSpec((tf, d), lambda i, j: (j, 0)),
                  pl.BlockSpec((tm, d), lambda i, j: (i, 0)),
                  pl.BlockSpec((1, d), lambda i, j: (0, 0))],
        out_specs=pl.BlockSpec((tm, d), lambda i, j: (i, 0)),
        out_shape=jax.ShapeDtypeStruct((n, d), F32),
        scratch_shapes=[pltpu.VMEM((tm, d), F32)],
        compiler_params=_cparams(("parallel", "arbitrary")),
        name="ffn_dense",
    )(x_bf, wg, wu, wd, h2d, g)


def _ffn_moe_kernel(be_ref, x_ref, wg_ref, wu_ref, wd_ref, o_ref, acc_ref):
    del be_ref
    _swiglu_step(x_ref[...], wg_ref, wu_ref, wd_ref, acc_ref)

    @pl.when(pl.program_id(1) == pl.num_programs(1) - 1)
    def _():
        _store_row_tiles(o_ref, acc_ref[...])


def _ffn_moe(block_expert, xs, wg, wu, wd, tb, tf):
    cap, d = xs.shape
    f = wg.shape[2]
    grid_spec = pltpu.PrefetchScalarGridSpec(
        num_scalar_prefetch=1,
        grid=(cap // tb, f // tf),
        in_specs=[pl.BlockSpec((tb, d), lambda i, j, be: (i, 0)),
                  pl.BlockSpec((None, d, tf), lambda i, j, be: (be[i], 0, j)),
                  pl.BlockSpec((None, d, tf), lambda i, j, be: (be[i], 0, j)),
                  pl.BlockSpec((None, tf, d), lambda i, j, be: (be[i], j, 0))],
        out_specs=pl.BlockSpec((tb * SUBLANES, LANES), lambda i, j, be: (i, 0)),
        scratch_shapes=[pltpu.VMEM((tb, d), F32)],
    )
    return pl.pallas_call(
        _ffn_moe_kernel,
        grid_spec=grid_spec,
        out_shape=jax.ShapeDtypeStruct((cap * SUBLANES, LANES), F32),
        compiler_params=_cparams(("parallel", "arbitrary")),
        name="ffn_moe",
    )(block_expert, xs, wg, wu, wd)


def _route_kernel(x_ref, wr_ref, tri_ref, oi_ref, ow_ref, cnt_ref, carry_ref):
    @pl.when(pl.program_id(0) == 0)
    def _():
        carry_ref[...] = jnp.zeros_like(carry_ref)

    logits = lax.dot_general(wr_ref[...], _load_row_tiles(x_ref), (((1,), (1,)), ((), ())),
                             precision=lax.Precision.HIGHEST, preferred_element_type=F32)
    shape = logits.shape
    eidx = lax.broadcasted_iota(jnp.int32, shape, 0)
    m1 = jnp.max(logits, axis=0, keepdims=True)
    i1 = jnp.min(jnp.where(logits == m1, eidx, N_EXPERTS), axis=0, keepdims=True)
    oh1 = eidx == i1
    rest = jnp.where(oh1, -jnp.inf, logits)
    m2 = jnp.max(rest, axis=0, keepdims=True)
    i2 = jnp.min(jnp.where(rest == m2, eidx, N_EXPERTS), axis=0, keepdims=True)
    oh2 = eidx == i2
    e = jnp.exp(m2 - m1)
    w1 = 1.0 / (1.0 + e)
    w2 = e * w1

    cnt = jnp.where(oh1, 1.0, jnp.where(oh2, 1.0, 0.0))
    incl = jnp.dot(cnt.astype(BF16), tri_ref[...], preferred_element_type=F32)
    excl = incl - cnt + carry_ref[:, 0:1]
    r1 = jnp.sum(jnp.where(oh1, excl, 0.0), axis=0, keepdims=True).astype(jnp.int32)
    r2 = jnp.sum(jnp.where(oh2, excl, 0.0), axis=0, keepdims=True).astype(jnp.int32)
    carry_ref[...] = carry_ref[...] + jnp.sum(cnt, axis=1, keepdims=True)

    bc = lambda v: jnp.broadcast_to(v, shape)
    oi_ref[...] = jnp.where(eidx == 0, bc(i1), jnp.where(eidx == 1, bc(i2),
                            jnp.where(eidx == 2, bc(r1), bc(r2))))
    ow_ref[...] = jnp.where(eidx == 0, bc(w1), bc(w2))
    cnt_ref[...] = carry_ref[...].astype(jnp.int32)


def _route(hn_tiles, wr_t, tr):
    n = hn_tiles.shape[0] // SUBLANES
    tri = (lax.broadcasted_iota(jnp.int32, (tr, tr), 0)
           <= lax.broadcasted_iota(jnp.int32, (tr, tr), 1)).astype(BF16)
    return pl.pallas_call(
        _route_kernel,
        grid=(n // tr,),
        in_specs=[pl.BlockSpec((tr * SUBLANES, LANES), lambda i: (i, 0)),
                  pl.BlockSpec(wr_t.shape, lambda i: (0, 0)),
                  pl.BlockSpec((tr, tr), lambda i: (0, 0))],
        out_specs=[pl.BlockSpec((N_EXPERTS, tr), lambda i: (0, i)),
                   pl.BlockSpec((N_EXPERTS, tr), lambda i: (0, i)),
                   pl.BlockSpec((N_EXPERTS, LANES), lambda i: (0, 0))],
        out_shape=[jax.ShapeDtypeStruct((N_EXPERTS, n), jnp.int32),
                   jax.ShapeDtypeStruct((N_EXPERTS, n), F32),
                   jax.ShapeDtypeStruct((N_EXPERTS, LANES), jnp.int32)],
        scratch_shapes=[pltpu.VMEM((N_EXPERTS, LANES), F32)],
        compiler_params=_cparams(("arbitrary",)),
        name="route",
    )(hn_tiles, wr_t, tri)


def _row_copy(src_hbm, dst_ref, sem, src_sublane, dst_sublane):
    return pltpu.make_async_copy(src_hbm.at[pl.ds(pl.multiple_of(src_sublane, SUBLANES), SUBLANES), :],
                                 dst_ref.at[pl.ds(pl.multiple_of(dst_sublane, SUBLANES), SUBLANES), :],
                                 sem)


def _start_rows(idx_ref, src_hbm, dst_ref, sem):
    rows = dst_ref.shape[0] // SUBLANES

    def group(g, c):
        base = g * ROW_UNROLL
        for u in range(ROW_UNROLL):
            _row_copy(src_hbm, dst_ref, sem, idx_ref[0, 0, base + u],
                      (base + u) * SUBLANES).start(priority=u % 2)
        return c

    lax.fori_loop(0, rows // ROW_UNROLL, group, 0)


def _wait_rows(src_hbm, dst_ref, sem):
    for r in range(dst_ref.shape[0] // SUBLANES):
        _row_copy(src_hbm, dst_ref, sem, 0, r * SUBLANES).wait()


def _ring_step(start, wait):
    i = pl.program_id(0)
    slot = i % 2

    @pl.when(i == 0)
    def _():
        start(0, False)

    @pl.when(i + 1 < pl.num_programs(0))
    def _():
        start(1 - slot, True)

    wait(slot)
    return slot


def _gather_kernel(idx_ref, idx_next_ref, src_hbm, o_ref, buf_ref, sem):
    def start(slot, nxt):
        _start_rows(idx_next_ref if nxt else idx_ref, src_hbm, buf_ref.at[slot], sem.at[slot])

    def wait(slot):
        _wait_rows(src_hbm, buf_ref.at[slot], sem.at[slot])

    slot = _ring_step(start, wait)
    o_ref[...] = _load_row_tiles(buf_ref.at[slot]).astype(o_ref.dtype)


def _next_block(nb):
    return lambda i: (jnp.minimum(i + 1, nb - 1), 0, 0)


def _gather(slot_src, src, rows):
    cap = slot_src.shape[0]
    nb = cap // rows
    idx3 = slot_src.reshape(nb, 1, rows)
    return pl.pallas_call(
        _gather_kernel,
        grid=(nb,),
        in_specs=[pl.BlockSpec((1, 1, rows), lambda i: (i, 0, 0), memory_space=pltpu.SMEM),
                  pl.BlockSpec((1, 1, rows), _next_block(nb), memory_space=pltpu.SMEM),
                  pl.BlockSpec(memory_space=pl.ANY)],
        out_specs=pl.BlockSpec((rows, D_MODEL), lambda i: (i, 0)),
        out_shape=jax.ShapeDtypeStruct((cap, D_MODEL), BF16),
        scratch_shapes=[pltpu.VMEM((2, rows * SUBLANES, LANES), src.dtype),
                        pltpu.SemaphoreType.DMA((2,))],
        compiler_params=_cparams(("arbitrary",)),
        name="moe_gather",
    )(idx3, idx3, src)


def _combine_kernel(d1_ref, d2_ref, d1n_ref, d2n_ref, w1_ref, w2_ref, h_ref, g_ref, y_hbm, o_ref,
                    y1_ref, y2_ref, sem):
    def start(slot, nxt):
        _start_rows(d1n_ref if nxt else d1_ref, y_hbm, y1_ref.at[slot], sem.at[0, slot])
        _start_rows(d2n_ref if nxt else d2_ref, y_hbm, y2_ref.at[slot], sem.at[1, slot])

    def wait(slot):
        _wait_rows(y_hbm, y1_ref.at[slot], sem.at[0, slot])
        _wait_rows(y_hbm, y2_ref.at[slot], sem.at[1, slot])

    slot = _ring_step(start, wait)
    f = (w1_ref[...] * _load_row_tiles(y1_ref.at[slot])
         + w2_ref[...] * _load_row_tiles(y2_ref.at[slot]))
    o_ref[...] = h_ref[...] + _rms(f, g_ref[...])


def _combine(dest1, dest2, w1, w2, h2d, g, y, tc):
    n, d = h2d.shape
    nb = n // tc
    idx = lambda v: v.reshape(nb, 1, tc)
    smem = pl.BlockSpec((1, 1, tc), lambda i: (i, 0, 0), memory_space=pltpu.SMEM)
    smem_next = pl.BlockSpec((1, 1, tc), _next_block(nb), memory_space=pltpu.SMEM)
    return pl.pallas_call(
        _combine_kernel,
        grid=(nb,),
        in_specs=[smem, smem, smem_next, smem_next,
                  pl.BlockSpec((tc, 1), lambda i: (i, 0)), pl.BlockSpec((tc, 1), lambda i: (i, 0)),
                  pl.BlockSpec((tc, d), lambda i: (i, 0)), pl.BlockSpec((1, d), lambda i: (0, 0)),
                  pl.BlockSpec(memory_space=pl.ANY)],
        out_specs=pl.BlockSpec((tc, d), lambda i: (i, 0)),
        out_shape=jax.ShapeDtypeStruct((n, d), F32),
        scratch_shapes=[pltpu.VMEM((2, tc * SUBLANES, LANES), y.dtype),
                        pltpu.VMEM((2, tc * SUBLANES, LANES), y.dtype),
                        pltpu.SemaphoreType.DMA((2, 2))],
        compiler_params=_cparams(("arbitrary",)),
        name="moe_combine",
    )(idx(dest1), idx(dest2), idx(dest1), idx(dest2), w1, w2, h2d, g, y)


def _moe(hn_tiles, h2d, w_router, wg, wu, wd, g_post, tiles):
    n = hn_tiles.shape[0] // SUBLANES
    tb = tiles["moe_rows"]
    oi, ow, cnt = _route(hn_tiles, w_router.T, tiles["route"])
    e1, e2, r1, r2 = oi[0], oi[1], oi[2], oi[3]
    counts = cnt[:, 0]
    padded = ((counts + tb - 1) // tb) * tb
    pad_end = jnp.cumsum(padded)
    pad_start = pad_end - padded
    dest1 = pad_start[e1] + r1
    dest2 = pad_start[e2] + r2
    n_blocks = -(-(2 * n) // tb) + N_EXPERTS
    tok = jnp.arange(n, dtype=jnp.int32) * SUBLANES
    slot_src = jnp.zeros((n_blocks * tb,), jnp.int32).at[jnp.concatenate([dest1, dest2])].set(
        jnp.concatenate([tok, tok]))
    block_start = jnp.arange(n_blocks, dtype=jnp.int32) * tb
    block_expert = jnp.minimum(jnp.searchsorted(pad_end, block_start, side="right"),
                               N_EXPERTS - 1).astype(jnp.int32)
    xs = _gather(slot_src, hn_tiles, tiles["gather"])
    y = _ffn_moe(block_expert, xs, wg, wu, wd, tb, tiles["ff"])
    return _combine(dest1 * SUBLANES, dest2 * SUBLANES, ow[0].reshape(n, 1), ow[1].reshape(n, 1),
                    h2d, g_post, y, tiles["combine"])


def _rope_tables(seq):
    t = jnp.arange(seq, dtype=jnp.int32)

    def cs(pos, dim):
        inv = ROPE_THETA ** (-jnp.arange(0, dim, 2, dtype=F32) / dim)
        ang = pos.astype(F32)[:, None] * inv[None, :]
        return jnp.cos(ang), jnp.sin(ang)

    rc, rs = cs(t // GRID_W, HEAD_DIM // 2)
    cc, cs_ = cs(t % GRID_W, HEAD_DIM // 2)
    sc, ss = cs(t, HEAD_DIM)
    cos_a = jnp.concatenate([rc, rc, cc, cc] * 2, axis=1)
    sin_a = jnp.concatenate([-rs, rs, -cs_, cs_] * 2, axis=1)
    cos_b = jnp.concatenate([sc, sc] * 2, axis=1)
    sin_b = jnp.concatenate([-ss, ss] * 2, axis=1)
    return cos_a, sin_a, cos_b, sin_b


def _tiles(n, seq, d_ff):
    pick = lambda total, want: math.gcd(total, want)
    return {
        "rows": pick(seq, 512),
        "tq_a": pick(seq, 128), "tq_b": pick(seq, 256), "tk": pick(seq, 512),
        "ffn_rows": pick(n, 1024), "ff": pick(d_ff, 512),
        "moe_rows": 1024 if n >= 4096 else 128, "route": pick(n, 512),
        "gather": pick(n, 512) if n >= 4096 else 128, "combine": pick(n, 256),
    }


def kernel(x, w_in, w_out, g_pre_mix, g_post_mix, g_pre_ffn, g_post_ffn, g_qnorm_a, g_knorm_a, g_out_a, diff_lambda, g_subln_b, w_gate_dense, w_up_dense, w_down_dense, w_router, w_gate_moe, w_up_moe, w_down_moe):
    b, s, d = x.shape
    assert d == D_MODEL, "row-tile layout needs one activation row per (8, 128) tile"
    n = b * s
    depth = w_in.shape[0]
    tiles = _tiles(n, s, w_gate_dense.shape[-1])
    tabs = _rope_tables(s)
    bd = (lax.broadcasted_iota(jnp.int32, (A_Q, A_Q), 0) // HEAD_DIM
          == lax.broadcasted_iota(jnp.int32, (A_Q, A_Q), 1) // HEAD_DIM).astype(BF16)
    row = lambda v: v.reshape(1, -1).astype(F32)

    h = x.reshape(n, d)
    for i in range(depth):
        gq = row(jnp.tile(g_qnorm_a[i], N_HEADS_A)) * Q_SCALE
        gk = row(jnp.tile(g_knorm_a[i], N_KV_A))
        qa, ka, va, qb, kb, vb = _inproj(h, row(g_pre_mix[i]), w_in[i].astype(BF16), gq, gk, tabs, bd,
                                         s, tiles["rows"])
        shp = lambda v: v.reshape(b, s, v.shape[-1])
        oa = _attn_a(shp(qa), shp(ka), shp(va), tiles["tq_a"], tiles["tk"])
        lam_init = 0.8 - 0.6 * math.exp(-0.3 * i)
        ob = _attn_b(diff_lambda[i].astype(F32), row(g_subln_b[i]), shp(qb), shp(kb), shp(vb),
                     tiles["tq_b"], tiles["tk"], lam_init)
        moe_layer = i % 2 == 1
        h, hn = _outproj(oa.reshape(n, A_Q), ob.reshape(n, B_V), h, w_out[i].astype(BF16),
                         row(g_out_a[i]), row(g_post_mix[i]), row(g_pre_ffn[i]),
                         moe_layer, tiles["rows"])
        j = i // 2
        if moe_layer:
            h = _moe(hn, h, w_router[j], w_gate_moe[j].astype(BF16), w_up_moe[j].astype(BF16),
                     w_down_moe[j].astype(BF16), row(g_post_ffn[i]), tiles)
        else:
            h = _ffn_dense(hn, w_gate_dense[j].astype(BF16), w_up_dense[j].astype(BF16),
                           w_down_dense[j].astype(BF16), h, row(g_post_ffn[i]),
                           tiles["ffn_rows"], tiles["ff"])
    return h.reshape(b, s, d)
```

```python
import functools
import math

import jax
import jax.numpy as jnp
from jax import lax
from jax.experimental import pallas as pl
from jax.experimental.pallas import tpu as pltpu

F32 = jnp.float32
BF16 = jnp.bfloat16

HEAD_DIM = 64
N_HEADS_A = 8
N_KV_A = 2
N_HEADS_B = 4
A_Q = N_HEADS_A * HEAD_DIM
A_KV = N_KV_A * HEAD_DIM
B_QK = N_HEADS_B * 2 * HEAD_DIM
B_V = N_HEADS_B * 2 * HEAD_DIM
GRID_W = 64
ROPE_THETA = 10000.0
EPS = 1e-6
N_EXPERTS = 8
LOG2E = 1.4426950408889634
LANES = 128
SUBLANES = 8
D_MODEL = SUBLANES * LANES
Q_SCALE = (HEAD_DIM ** -0.5) * LOG2E

VMEM_LIMIT = 56 * 1024 * 1024
ROW_UNROLL = 8
Q_TILES = 2


def _cparams(sem):
    return pltpu.CompilerParams(dimension_semantics=sem, vmem_limit_bytes=VMEM_LIMIT)


def _rms(x, g):
    return x * lax.rsqrt(jnp.mean(x * x, axis=-1, keepdims=True) + EPS) * g


def _lane(shape):
    return lax.broadcasted_iota(jnp.int32, shape, len(shape) - 1)


def _tile4(t):
    return jnp.concatenate([t, t, t, t], axis=1)


def _store_row_tiles(ref, x):
    rows = x.shape[0]
    for c in range(SUBLANES):
        ref[pl.ds(c, rows, stride=SUBLANES), :] = x[:, c * LANES:(c + 1) * LANES]


def _load_row_tiles(ref):
    rows = ref.shape[0] // SUBLANES
    return jnp.concatenate([ref[pl.ds(c, rows, stride=SUBLANES), :] for c in range(SUBLANES)], axis=1)


def _rotate_pairs(x, half):
    w = x.shape[-1]
    first = (_lane(x.shape) % (2 * half)) < half
    return jnp.where(first, pltpu.roll(x, w - half, axis=1), pltpu.roll(x, half, axis=1))


def _inproj_kernel(h_ref, g_ref, w_ref, gq_ref, gk_ref, ca_ref, sa_ref, cb_ref, sb_ref, bd_ref,
                   qa_ref, ka_ref, va_ref, qb_ref, kb_ref, vb_ref):
    hn = _rms(h_ref[...], g_ref[...]).astype(BF16)
    tm = hn.shape[0]
    lane = _lane((tm, LANES))
    low = lane < HEAD_DIM
    ca, sa = ca_ref[...], sa_ref[...]
    cb, sb = cb_ref[...], sb_ref[...]

    def proj(lo, hi):
        return jnp.dot(hn, w_ref[:, lo:hi], preferred_element_type=F32)

    def head_norm(x, bd, g):
        ms = jnp.dot((x * x).astype(BF16), bd, preferred_element_type=F32) * (1.0 / HEAD_DIM)
        return x * lax.rsqrt(ms + EPS) * g

    qa = head_norm(proj(0, A_Q), bd_ref[...], gq_ref[...])
    qa = qa * _tile4(ca) + _rotate_pairs(qa, HEAD_DIM // 4) * _tile4(sa)
    for i in range(N_HEADS_A // 2):
        slab = qa[:, i * LANES:(i + 1) * LANES]
        qa_ref[:, (2 * i) * LANES:(2 * i + 1) * LANES] = jnp.where(low, slab, 0.0).astype(BF16)
        qa_ref[:, (2 * i + 1) * LANES:(2 * i + 2) * LANES] = jnp.where(
            low, pltpu.roll(slab, HEAD_DIM, axis=1), 0.0).astype(BF16)

    ka = head_norm(proj(A_Q, A_Q + A_KV), bd_ref[0:LANES, 0:LANES], gk_ref[...])
    ka = ka * ca + _rotate_pairs(ka, HEAD_DIM // 4) * sa
    ka_ref[:, 0:LANES] = jnp.where(low, ka, 0.0).astype(BF16)
    ka_ref[:, LANES:2 * LANES] = jnp.where(low, pltpu.roll(ka, HEAD_DIM, axis=1), 0.0).astype(BF16)

    va = proj(A_Q + A_KV, A_Q + 2 * A_KV)
    one_col = jnp.where(lane == HEAD_DIM, 1.0, 0.0)
    va_ref[:, 0:LANES] = jnp.where(low, va, one_col).astype(BF16)
    va_ref[:, LANES:2 * LANES] = jnp.where(low, pltpu.roll(va, HEAD_DIM, axis=1), one_col).astype(BF16)

    o = A_Q + 2 * A_KV
    qb = proj(o, o + B_QK)
    qb = (qb * _tile4(cb) + _rotate_pairs(qb, HEAD_DIM // 2) * _tile4(sb)) * Q_SCALE
    qb_ref[...] = qb.astype(BF16)
    kb = proj(o + B_QK, o + 2 * B_QK)
    kb = kb * _tile4(cb) + _rotate_pairs(kb, HEAD_DIM // 2) * _tile4(sb)
    kb_ref[...] = kb.astype(BF16)
    vb = proj(o + 2 * B_QK, o + 2 * B_QK + B_V)
    first_col = jnp.where(lane == 0, 1.0, 0.0).astype(BF16)
    for hd in range(N_HEADS_B):
        vb_ref[:, (2 * hd) * LANES:(2 * hd + 1) * LANES] = vb[:, hd * LANES:(hd + 1) * LANES].astype(BF16)
        vb_ref[:, (2 * hd + 1) * LANES:(2 * hd + 2) * LANES] = first_col


def _inproj(h2d, g, w_bf, gq, gk, tabs, bd, seq, tm):
    n, d = h2d.shape
    nt = seq // tm
    ca, sa, cb, sb = tabs
    row = lambda i: (i, 0)
    const = lambda i: (0, 0)
    tab = lambda i: (i % nt, 0)
    widths = (N_HEADS_A * LANES, N_KV_A * LANES, N_KV_A * LANES, B_QK, B_QK, 2 * B_V)
    return pl.pallas_call(
        _inproj_kernel,
        grid=(n // tm,),
        in_specs=[pl.BlockSpec((tm, d), row), pl.BlockSpec((1, d), const),
                  pl.BlockSpec(w_bf.shape, const), pl.BlockSpec((1, A_Q), const),
                  pl.BlockSpec((1, A_KV), const)]
                 + [pl.BlockSpec((tm, LANES), tab)] * 4
                 + [pl.BlockSpec(bd.shape, const)],
        out_specs=[pl.BlockSpec((tm, w), row) for w in widths],
        out_shape=[jax.ShapeDtypeStruct((n, w), BF16) for w in widths],
        compiler_params=_cparams(("parallel",)),
        name="inproj",
    )(h2d, g, w_bf, gq, gk, ca, sa, cb, sb, bd)


def _flash(qs, k_ref, v_ref, tk):
    nk = k_ref.shape[0] // tk

    def scores(q, c):
        return lax.dot_general(q, k_ref[c * tk:(c + 1) * tk, :], (((1,), (1,)), ((), ())),
                               preferred_element_type=F32)

    ms = [jnp.full((q.shape[0], 1), -jnp.inf, F32) for q in qs]
    accs = [jnp.zeros((q.shape[0], v_ref.shape[1]), F32) for q in qs]
    ss = [scores(q, 0) for q in qs]
    for c in range(nk):
        nxt = [scores(q, c + 1) if c + 1 < nk else None for q in qs]
        for t, s in enumerate(ss):
            m_new = jnp.maximum(ms[t], jnp.max(s, axis=-1, keepdims=True))
            p = jnp.exp2((s - m_new).astype(BF16))
            accs[t] = jnp.exp2(ms[t] - m_new) * accs[t] + jnp.dot(
                p, v_ref[c * tk:(c + 1) * tk, :], preferred_element_type=F32)
            ms[t] = m_new
        ss = nxt
    return accs


def _attn_a_kernel(q_ref, k_ref, v_ref, o_ref, *, tq, tk):
    g = N_HEADS_A // N_KV_A
    tiles = range(q_ref.shape[0] // tq)
    qs = [jnp.concatenate([q_ref[t * tq:(t + 1) * tq, j * LANES:(j + 1) * LANES] for j in range(g)],
                          axis=0) for t in tiles]
    low = _lane((tq, LANES)) < HEAD_DIM
    for t, acc in zip(tiles, _flash(qs, k_ref, v_ref, tk)):
        o = acc * (1.0 / acc[:, HEAD_DIM:HEAD_DIM + 1])
        slabs = [jnp.where(low, o[(2 * i) * tq:(2 * i + 1) * tq],
                           pltpu.roll(o[(2 * i + 1) * tq:(2 * i + 2) * tq], HEAD_DIM, axis=1))
                 for i in range(g // 2)]
        o_ref[t * tq:(t + 1) * tq, :] = jnp.concatenate(slabs, axis=1).astype(o_ref.dtype)


def _attn_a(qa, ka, va, tq, tk):
    b, s, _ = qa.shape
    g = N_HEADS_A // N_KV_A
    rows = tq * Q_TILES
    return pl.pallas_call(
        functools.partial(_attn_a_kernel, tq=tq, tk=tk),
        grid=(b, N_KV_A, s // rows),
        in_specs=[pl.BlockSpec((None, rows, g * LANES), lambda bi, h, i: (bi, i, h)),
                  pl.BlockSpec((None, s, LANES), lambda bi, h, i: (bi, 0, h)),
                  pl.BlockSpec((None, s, LANES), lambda bi, h, i: (bi, 0, h))],
        out_specs=pl.BlockSpec((None, rows, g * HEAD_DIM), lambda bi, h, i: (bi, i, h)),
        out_shape=jax.ShapeDtypeStruct((b, s, A_Q), BF16),
        compiler_params=_cparams(("parallel", "parallel", "parallel")),
        name="attn_a",
    )(qa, ka, va)


def _attn_b_kernel(lam_ref, gs_ref, q_ref, k_ref, v_ref, o_ref, *, tq, tk, lam_init):
    tiles = range(q_ref.shape[0] // tq)
    low = _lane((tq, LANES)) < HEAD_DIM
    zero = jnp.zeros((tq, LANES), q_ref.dtype)
    qs = []
    for t in tiles:
        q = q_ref[t * tq:(t + 1) * tq, :]
        qs.append(jnp.concatenate([jnp.where(low, q, zero), jnp.where(low, zero, q)], axis=0))
    lp = lam_ref[...]
    lam = (jnp.exp(jnp.sum(lp[0:1] * lp[1:2], axis=-1, keepdims=True))
           - jnp.exp(jnp.sum(lp[2:3] * lp[3:4], axis=-1, keepdims=True)) + lam_init)
    w = 2 * HEAD_DIM
    for t, acc in zip(tiles, _flash(qs, k_ref, v_ref, tk)):
        o = acc[:, 0:w] * (1.0 / acc[:, w:w + 1])
        d = o[0:tq] - lam * o[tq:2 * tq]
        o_ref[t * tq:(t + 1) * tq, :] = (_rms(d, gs_ref[...]) * (1.0 - lam_init)).astype(o_ref.dtype)


def _attn_b(lam_p, gs, qb, kb, vb, tq, tk, lam_init):
    b, s, _ = qb.shape
    rows = tq * Q_TILES
    return pl.pallas_call(
        functools.partial(_attn_b_kernel, tq=tq, tk=tk, lam_init=lam_init),
        grid=(b, N_HEADS_B, s // rows),
        in_specs=[pl.BlockSpec(lam_p.shape, lambda bi, h, i: (0, 0)),
                  pl.BlockSpec(gs.shape, lambda bi, h, i: (0, 0)),
                  pl.BlockSpec((None, rows, LANES), lambda bi, h, i: (bi, i, h)),
                  pl.BlockSpec((None, s, LANES), lambda bi, h, i: (bi, 0, h)),
                  pl.BlockSpec((None, s, 2 * LANES), lambda bi, h, i: (bi, 0, h))],
        out_specs=pl.BlockSpec((None, rows, LANES), lambda bi, h, i: (bi, i, h)),
        out_shape=jax.ShapeDtypeStruct((b, s, B_V), BF16),
        compiler_params=_cparams(("parallel", "parallel", "parallel")),
        name="attn_b",
    )(lam_p, gs, qb, kb, vb)


def _outproj_kernel(oa_ref, ob_ref, h_ref, w_ref, ga_ref, gpost_ref, gpre_ref, h2_ref, hn_ref):
    a = _rms(oa_ref[...].astype(F32), ga_ref[...]).astype(BF16)
    mix = (jnp.dot(a, w_ref[0:A_Q, :], preferred_element_type=F32)
           + jnp.dot(ob_ref[...], w_ref[A_Q:, :], preferred_element_type=F32))
    h2 = h_ref[...] + _rms(mix, gpost_ref[...])
    h2_ref[...] = h2
    hn = _rms(h2, gpre_ref[...])
    if hn_ref.shape[1] == LANES:
        _store_row_tiles(hn_ref, hn)
    else:
        hn_ref[...] = hn.astype(hn_ref.dtype)


def _outproj(oa, ob, h2d, w_bf, ga, gpost, gpre, row_tiles, tm):
    n, d = h2d.shape
    row = lambda i: (i, 0)
    const = lambda i: (0, 0)
    if row_tiles:
        hn_spec = pl.BlockSpec((tm * SUBLANES, LANES), row)
        hn_shape = jax.ShapeDtypeStruct((n * SUBLANES, LANES), F32)
    else:
        hn_spec = pl.BlockSpec((tm, d), row)
        hn_shape = jax.ShapeDtypeStruct((n, d), BF16)
    return pl.pallas_call(
        _outproj_kernel,
        grid=(n // tm,),
        in_specs=[pl.BlockSpec((tm, A_Q), row), pl.BlockSpec((tm, B_V), row),
                  pl.BlockSpec((tm, d), row), pl.BlockSpec(w_bf.shape, const),
                  pl.BlockSpec((1, A_Q), const), pl.BlockSpec((1, d), const),
                  pl.BlockSpec((1, d), const)],
        out_specs=[pl.BlockSpec((tm, d), row), hn_spec],
        out_shape=[jax.ShapeDtypeStruct((n, d), F32), hn_shape],
        compiler_params=_cparams(("parallel",)),
        name="outproj",
    )(oa, ob, h2d, w_bf, ga, gpost, gpre)


def _swiglu_step(x_ref, wg_ref, wu_ref, wd_ref, acc_ref):
    j = pl.program_id(1)

    @pl.when(j == 0)
    def _():
        acc_ref[...] = jnp.zeros_like(acc_ref)

    gate = jnp.dot(x_ref[...], wg_ref[...], preferred_element_type=F32)
    up = jnp.dot(x_ref[...], wu_ref[...], preferred_element_type=F32)
    act = (gate * jax.nn.sigmoid(gate) * up).astype(BF16)
    acc_ref[...] += jnp.dot(act, wd_ref[...], preferred_element_type=F32)


def _ffn_dense_kernel(x_ref, wg_ref, wu_ref, wd_ref, h_ref, g_ref, o_ref, acc_ref):
    _swiglu_step(x_ref, wg_ref, wu_ref, wd_ref, acc_ref)

    @pl.when(pl.program_id(1) == pl.num_programs(1) - 1)
    def _():
        o_ref[...] = h_ref[...] + _rms(acc_ref[...], g_ref[...])


def _ffn_dense(x_bf, wg, wu, wd, h2d, g, tm, tf):
    n, d = h2d.shape
    f = wg.shape[1]
    return pl.pallas_call(
        _ffn_dense_kernel,
        grid=(n // tm, f // tf),
        in_specs=[pl.BlockSpec((tm, d), lambda i, j: (i, 0)),
                  pl.BlockSpec((d, tf), lambda i, j: (0, j)),
                  pl.BlockSpec((d, tf), lambda i, j: (0, j)),
                  pl.BlockSpec((tf, d), lambda i, j: (j, 0)),
                  pl.BlockSpec((tm, d), lambda i, j: (i, 0)),
                  pl.BlockSpec((1, d), lambda i, j: (0, 0))],
        out_specs=pl.BlockSpec((tm, d), lambda i, j: (i, 0)),
        out_shape=jax.ShapeDtypeStruct((n, d), F32),
        scratch_shapes=[pltpu.VMEM((tm, d), F32)],
        compiler_params=_cparams(("parallel", "arbitrary")),
        name="ffn_dense",
    )(x_bf, wg, wu, wd, h2d, g)


def _ffn_moe_kernel(be_ref, x_ref, wg_ref, wu_ref, wd_ref, o_ref, acc_ref):
    del be_ref
    _swiglu_step(x_ref, wg_ref, wu_ref, wd_ref, acc_ref)

    @pl.when(pl.program_id(1) == pl.num_programs(1) - 1)
    def _():
        _store_row_tiles(o_ref, acc_ref[...])


def _ffn_moe(block_expert, xs, wg, wu, wd, tb, tf):
    cap, d = xs.shape
    f = wg.shape[2]
    grid_spec = pltpu.PrefetchScalarGridSpec(
        num_scalar_prefetch=1,
        grid=(cap // tb, f // tf),
        in_specs=[pl.BlockSpec((tb, d), lambda i, j, be: (i, 0)),
                  pl.BlockSpec((None, d, tf), lambda i, j, be: (be[i], 0, j)),
                  pl.BlockSpec((None, d, tf), lambda i, j, be: (be[i], 0, j)),
                  pl.BlockSpec((None, tf, d), lambda i, j, be: (be[i], j, 0))],
        out_specs=pl.BlockSpec((tb * SUBLANES, LANES), lambda i, j, be: (i, 0)),
        scratch_shapes=[pltpu.VMEM((tb, d), F32)],
    )
    return pl.pallas_call(
        _ffn_moe_kernel,
        grid_spec=grid_spec,
        out_shape=jax.ShapeDtypeStruct((cap * SUBLANES, LANES), F32),
        compiler_params=_cparams(("parallel", "arbitrary")),
        name="ffn_moe",
    )(block_expert, xs, wg, wu, wd)


def _route_kernel(x_ref, wr_ref, tri_ref, oi_ref, ow_ref, cnt_ref, carry_ref):
    @pl.when(pl.program_id(0) == 0)
    def _():
        carry_ref[...] = jnp.zeros_like(carry_ref)

    logits = lax.dot_general(wr_ref[...], _load_row_tiles(x_ref), (((1,), (1,)), ((), ())),
                             precision=lax.Precision.HIGHEST, preferred_element_type=F32)
    shape = logits.shape
    eidx = lax.broadcasted_iota(jnp.int32, shape, 0)
    m1 = jnp.max(logits, axis=0, keepdims=True)
    i1 = jnp.min(jnp.where(logits == m1, eidx, N_EXPERTS), axis=0, keepdims=True)
    oh1 = eidx == i1
    rest = jnp.where(oh1, -jnp.inf, logits)
    m2 = jnp.max(rest, axis=0, keepdims=True)
    i2 = jnp.min(jnp.where(rest == m2, eidx, N_EXPERTS), axis=0, keepdims=True)
    oh2 = eidx == i2
    e = jnp.exp(m2 - m1)
    w1 = 1.0 / (1.0 + e)
    w2 = e * w1

    cnt = jnp.where(oh1, 1.0, jnp.where(oh2, 1.0, 0.0))
    incl = jnp.dot(cnt.astype(BF16), tri_ref[...], preferred_element_type=F32)
    excl = incl - cnt + carry_ref[:, 0:1]
    r1 = jnp.sum(jnp.where(oh1, excl, 0.0), axis=0, keepdims=True).astype(jnp.int32)
    r2 = jnp.sum(jnp.where(oh2, excl, 0.0), axis=0, keepdims=True).astype(jnp.int32)
    carry_ref[...] = carry_ref[...] + jnp.sum(cnt, axis=1, keepdims=True)

    bc = lambda v: jnp.broadcast_to(v, shape)
    oi_ref[...] = jnp.where(eidx == 0, bc(i1), jnp.where(eidx == 1, bc(i2),
                            jnp.where(eidx == 2, bc(r1), bc(r2))))
    ow_ref[...] = jnp.where(eidx == 0, bc(w1), bc(w2))
    cnt_ref[...] = carry_ref[...].astype(jnp.int32)


def _route(hn_tiles, wr_t, tr):
    n = hn_tiles.shape[0] // SUBLANES
    tri = (lax.broadcasted_iota(jnp.int32, (tr, tr), 0)
           <= lax.broadcasted_iota(jnp.int32, (tr, tr), 1)).astype(BF16)
    return pl.pallas_call(
        _route_kernel,
        grid=(n // tr,),
        in_specs=[pl.BlockSpec((tr * SUBLANES, LANES), lambda i: (i, 0)),
                  pl.BlockSpec(wr_t.shape, lambda i: (0, 0)),
                  pl.BlockSpec((tr, tr), lambda i: (0, 0))],
        out_specs=[pl.BlockSpec((N_EXPERTS, tr), lambda i: (0, i)),
                   pl.BlockSpec((N_EXPERTS, tr), lambda i: (0, i)),
                   pl.BlockSpec((N_EXPERTS, LANES), lambda i: (0, 0))],
        out_shape=[jax.ShapeDtypeStruct((N_EXPERTS, n), jnp.int32),
                   jax.ShapeDtypeStruct((N_EXPERTS, n), F32),
                   jax.ShapeDtypeStruct((N_EXPERTS, LANES), jnp.int32)],
        scratch_shapes=[pltpu.VMEM((N_EXPERTS, LANES), F32)],
        compiler_params=_cparams(("arbitrary",)),
        name="route",
    )(hn_tiles, wr_t, tri)


def _row_copy(src_hbm, dst_ref, sem, src_sublane, dst_sublane):
    return pltpu.make_async_copy(src_hbm.at[pl.ds(pl.multiple_of(src_sublane, SUBLANES), SUBLANES), :],
                                 dst_ref.at[pl.ds(pl.multiple_of(dst_sublane, SUBLANES), SUBLANES), :],
                                 sem)


def _start_rows(idx_ref, src_hbm, dst_ref, sem):
    rows = dst_ref.shape[0] // SUBLANES

    def group(g, c):
        base = g * ROW_UNROLL
        for u in range(ROW_UNROLL):
            _row_copy(src_hbm, dst_ref, sem, idx_ref[0, 0, base + u],
                      (base + u) * SUBLANES).start(priority=u % 2)
        return c

    lax.fori_loop(0, rows // ROW_UNROLL, group, 0)


def _wait_rows(src_hbm, dst_ref, sem):
    for r in range(dst_ref.shape[0] // SUBLANES):
        _row_copy(src_hbm, dst_ref, sem, 0, r * SUBLANES).wait()


def _ring_step(start, wait):
    i = pl.program_id(0)
    slot = i % 2

    @pl.when(i == 0)
    def _():
        start(0, False)

    @pl.when(i + 1 < pl.num_programs(0))
    def _():
        start(1 - slot, True)

    wait(slot)
    return slot


def _gather_kernel(idx_ref, idx_next_ref, src_hbm, o_ref, buf_ref, sem):
    def start(slot, nxt):
        _start_rows(idx_next_ref if nxt else idx_ref, src_hbm, buf_ref.at[slot], sem.at[slot])

    def wait(slot):
        _wait_rows(src_hbm, buf_ref.at[slot], sem.at[slot])

    slot = _ring_step(start, wait)
    o_ref[...] = _load_row_tiles(buf_ref.at[slot]).astype(o_ref.dtype)


def _next_block(nb):
    return lambda i: (jnp.minimum(i + 1, nb - 1), 0, 0)


def _gather(slot_src, src, rows):
    cap = slot_src.shape[0]
    nb = cap // rows
    idx3 = slot_src.reshape(nb, 1, rows)
    return pl.pallas_call(
        _gather_kernel,
        grid=(nb,),
        in_specs=[pl.BlockSpec((1, 1, rows), lambda i: (i, 0, 0), memory_space=pltpu.SMEM),
                  pl.BlockSpec((1, 1, rows), _next_block(nb), memory_space=pltpu.SMEM),
                  pl.BlockSpec(memory_space=pl.ANY)],
        out_specs=pl.BlockSpec((rows, D_MODEL), lambda i: (i, 0)),
        out_shape=jax.ShapeDtypeStruct((cap, D_MODEL), BF16),
        scratch_shapes=[pltpu.VMEM((2, rows * SUBLANES, LANES), src.dtype),
                        pltpu.SemaphoreType.DMA((2,))],
        compiler_params=_cparams(("arbitrary",)),
        name="moe_gather",
    )(idx3, idx3, src)


def _combine_kernel(d1_ref, d2_ref, d1n_ref, d2n_ref, w1_ref, w2_ref, h_ref, g_ref, y_hbm, o_ref,
                    y1_ref, y2_ref, sem):
    def start(slot, nxt):
        _start_rows(d1n_ref if nxt else d1_ref, y_hbm, y1_ref.at[slot], sem.at[0, slot])
        _start_rows(d2n_ref if nxt else d2_ref, y_hbm, y2_ref.at[slot], sem.at[1, slot])

    def wait(slot):
        _wait_rows(y_hbm, y1_ref.at[slot], sem.at[0, slot])
        _wait_rows(y_hbm, y2_ref.at[slot], sem.at[1, slot])

    slot = _ring_step(start, wait)
    f = (w1_ref[...] * _load_row_tiles(y1_ref.at[slot])
         + w2_ref[...] * _load_row_tiles(y2_ref.at[slot]))
    o_ref[...] = h_ref[...] + _rms(f, g_ref[...])


def _combine(dest1, dest2, w1, w2, h2d, g, y, tc):
    n, d = h2d.shape
    nb = n // tc
    idx = lambda v: v.reshape(nb, 1, tc)
    smem = pl.BlockSpec((1, 1, tc), lambda i: (i, 0, 0), memory_space=pltpu.SMEM)
    smem_next = pl.BlockSpec((1, 1, tc), _next_block(nb), memory_space=pltpu.SMEM)
    return pl.pallas_call(
        _combine_kernel,
        grid=(nb,),
        in_specs=[smem, smem, smem_next, smem_next,
                  pl.BlockSpec((tc, 1), lambda i: (i, 0)), pl.BlockSpec((tc, 1), lambda i: (i, 0)),
                  pl.BlockSpec((tc, d), lambda i: (i, 0)), pl.BlockSpec((1, d), lambda i: (0, 0)),
                  pl.BlockSpec(memory_space=pl.ANY)],
        out_specs=pl.BlockSpec((tc, d), lambda i: (i, 0)),
        out_shape=jax.ShapeDtypeStruct((n, d), F32),
        scratch_shapes=[pltpu.VMEM((2, tc * SUBLANES, LANES), y.dtype),
                        pltpu.VMEM((2, tc * SUBLANES, LANES), y.dtype),
                        pltpu.SemaphoreType.DMA((2, 2))],
        compiler_params=_cparams(("arbitrary",)),
        name="moe_combine",
    )(idx(dest1), idx(dest2), idx(dest1), idx(dest2), w1, w2, h2d, g, y)


def _moe(hn_tiles, h2d, w_router, wg, wu, wd, g_post, tiles):
    n = hn_tiles.shape[0] // SUBLANES
    tb = tiles["moe_rows"]
    oi, ow, cnt = _route(hn_tiles, w_router.T, tiles["route"])
    e1, e2, r1, r2 = oi[0], oi[1], oi[2], oi[3]
    counts = cnt[:, 0]
    padded = ((counts + tb - 1) // tb) * tb
    pad_end = jnp.cumsum(padded)
    pad_start = pad_end - padded
    dest1 = pad_start[e1] + r1
    dest2 = pad_start[e2] + r2
    n_blocks = -(-(2 * n) // tb) + N_EXPERTS
    tok = jnp.arange(n, dtype=jnp.int32) * SUBLANES
    slot_src = jnp.zeros((n_blocks * tb,), jnp.int32).at[jnp.concatenate([dest1, dest2])].set(
        jnp.concatenate([tok, tok]))
    block_start = jnp.arange(n_blocks, dtype=jnp.int32) * tb
    block_expert = jnp.minimum(jnp.searchsorted(pad_end, block_start, side="right"),
                               N_EXPERTS - 1).astype(jnp.int32)
    xs = _gather(slot_src, hn_tiles, tiles["gather"])
    y = _ffn_moe(block_expert, xs, wg, wu, wd, tb, tiles["ff"])
    return _combine(dest1 * SUBLANES, dest2 * SUBLANES, ow[0].reshape(n, 1), ow[1].reshape(n, 1),
                    h2d, g_post, y, tiles["combine"])


def _rope_tables(seq):
    t = jnp.arange(seq, dtype=jnp.int32)

    def cs(pos, dim):
        inv = ROPE_THETA ** (-jnp.arange(0, dim, 2, dtype=F32) / dim)
        ang = pos.astype(F32)[:, None] * inv[None, :]
        return jnp.cos(ang), jnp.sin(ang)

    rc, rs = cs(t // GRID_W, HEAD_DIM // 2)
    cc, cs_ = cs(t % GRID_W, HEAD_DIM // 2)
    sc, ss = cs(t, HEAD_DIM)
    cos_a = jnp.concatenate([rc, rc, cc, cc] * 2, axis=1)
    sin_a = jnp.concatenate([-rs, rs, -cs_, cs_] * 2, axis=1)
    cos_b = jnp.concatenate([sc, sc] * 2, axis=1)
    sin_b = jnp.concatenate([-ss, ss] * 2, axis=1)
    return cos_a, sin_a, cos_b, sin_b


def _tiles(n, seq, d_ff):
    pick = lambda total, want: math.gcd(total, want)
    return {
        "rows": pick(seq, 512),
        "tq_a": pick(seq, 128), "tq_b": pick(seq, 256), "tk": pick(seq, 512),
        "ffn_rows": pick(n, 1024), "ff": pick(d_ff, 512),
        "moe_rows": 1024 if n >= 4096 else 128, "route": pick(n, 512),
        "gather": pick(n, 512) if n >= 4096 else 128, "combine": pick(n, 256),
    }


def kernel(x, w_in, w_out, g_pre_mix, g_post_mix, g_pre_ffn, g_post_ffn, g_qnorm_a, g_knorm_a, g_out_a, diff_lambda, g_subln_b, w_gate_dense, w_up_dense, w_down_dense, w_router, w_gate_moe, w_up_moe, w_down_moe):
    b, s, d = x.shape
    assert d == D_MODEL, "row-tile layout needs one activation row per (8, 128) tile"
    n = b * s
    depth = w_in.shape[0]
    tiles = _tiles(n, s, w_gate_dense.shape[-1])
    tabs = _rope_tables(s)
    bd = (lax.broadcasted_iota(jnp.int32, (A_Q, A_Q), 0) // HEAD_DIM
          == lax.broadcasted_iota(jnp.int32, (A_Q, A_Q), 1) // HEAD_DIM).astype(BF16)
    row = lambda v: v.reshape(1, -1).astype(F32)

    h = x.reshape(n, d)
    for i in range(depth):
        gq = row(jnp.tile(g_qnorm_a[i], N_HEADS_A)) * Q_SCALE
        gk = row(jnp.tile(g_knorm_a[i], N_KV_A))
        qa, ka, va, qb, kb, vb = _inproj(h, row(g_pre_mix[i]), w_in[i].astype(BF16), gq, gk, tabs, bd,
                                         s, tiles["rows"])
        shp = lambda v: v.reshape(b, s, v.shape[-1])
        oa = _attn_a(shp(qa), shp(ka), shp(va), tiles["tq_a"], tiles["tk"])
        lam_init = 0.8 - 0.6 * math.exp(-0.3 * i)
        ob = _attn_b(diff_lambda[i].astype(F32), row(g_subln_b[i]), shp(qb), shp(kb), shp(vb),
                     tiles["tq_b"], tiles["tk"], lam_init)
        moe_layer = i % 2 == 1
        h, hn = _outproj(oa.reshape(n, A_Q), ob.reshape(n, B_V), h, w_out[i].astype(BF16),
                         row(g_out_a[i]), row(g_post_mix[i]), row(g_pre_ffn[i]),
                         moe_layer, tiles["rows"])
        j = i // 2
        if moe_layer:
            h = _moe(hn, h, w_router[j], w_gate_moe[j].astype(BF16), w_up_moe[j].astype(BF16),
                     w_down_moe[j].astype(BF16), row(g_post_ffn[i]), tiles)
        else:
            h = _ffn_dense(hn, w_gate_dense[j].astype(BF16), w_up_dense[j].astype(BF16),
                           w_down_dense[j].astype(BF16), h, row(g_post_ffn[i]),
                           tiles["ffn_rows"], tiles["ff"])
    return h.reshape(b, s, d)
```

```python
import functools
import math

import jax
import jax.numpy as jnp
from jax import lax
from jax.experimental import pallas as pl
from jax.experimental.pallas import tpu as pltpu

F32 = jnp.float32
BF16 = jnp.bfloat16

HEAD_DIM = 64
N_HEADS_A = 8
N_KV_A = 2
N_HEADS_B = 4
A_Q = N_HEADS_A * HEAD_DIM
A_KV = N_KV_A * HEAD_DIM
B_QK = N_HEADS_B * 2 * HEAD_DIM
B_V = N_HEADS_B * 2 * HEAD_DIM
GRID_W = 64
ROPE_THETA = 10000.0
EPS = 1e-6
N_EXPERTS = 8
LOG2E = 1.4426950408889634
LANES = 128
SUBLANES = 8
D_MODEL = SUBLANES * LANES
Q_SCALE = (HEAD_DIM ** -0.5) * LOG2E

VMEM_LIMIT = 56 * 1024 * 1024
ROW_UNROLL = 8
Q_TILES = 2


def _cparams(sem):
    return pltpu.CompilerParams(dimension_semantics=sem, vmem_limit_bytes=VMEM_LIMIT)


def _rms(x, g):
    return x * lax.rsqrt(jnp.mean(x * x, axis=-1, keepdims=True) + EPS) * g


def _lane(shape):
    return lax.broadcasted_iota(jnp.int32, shape, len(shape) - 1)


def _tile4(t):
    return jnp.concatenate([t, t, t, t], axis=1)


def _store_row_tiles(ref, x):
    rows = x.shape[0]
    for c in range(SUBLANES):
        ref[pl.ds(c, rows, stride=SUBLANES), :] = x[:, c * LANES:(c + 1) * LANES]


def _load_row_tiles(ref):
    rows = ref.shape[0] // SUBLANES
    return jnp.concatenate([ref[pl.ds(c, rows, stride=SUBLANES), :] for c in range(SUBLANES)], axis=1)


def _rotate_pairs(x, half):
    w = x.shape[-1]
    first = (_lane(x.shape) % (2 * half)) < half
    return jnp.where(first, pltpu.roll(x, w - half, axis=1), pltpu.roll(x, half, axis=1))


def _inproj_kernel(h_ref, g_ref, w_ref, gq_ref, gk_ref, ca_ref, sa_ref, cb_ref, sb_ref, bd_ref,
                   qa_ref, ka_ref, va_ref, qb_ref, kb_ref, vb_ref):
    hn = _rms(h_ref[...], g_ref[...]).astype(BF16)
    tm = hn.shape[0]
    lane = _lane((tm, LANES))
    low = lane < HEAD_DIM
    ca, sa = ca_ref[...], sa_ref[...]
    cb, sb = cb_ref[...], sb_ref[...]

    def proj(lo, hi):
        return jnp.dot(hn, w_ref[:, lo:hi], preferred_element_type=F32)

    def head_norm(x, bd, g):
        ms = jnp.dot((x * x).astype(BF16), bd, preferred_element_type=F32) * (1.0 / HEAD_DIM)
        return x * lax.rsqrt(ms + EPS) * g

    qa = head_norm(proj(0, A_Q), bd_ref[...], gq_ref[...])
    qa = qa * _tile4(ca) + _rotate_pairs(qa, HEAD_DIM // 4) * _tile4(sa)
    for i in range(N_HEADS_A // 2):
        slab = qa[:, i * LANES:(i + 1) * LANES]
        qa_ref[:, (2 * i) * LANES:(2 * i + 1) * LANES] = jnp.where(low, slab, 0.0).astype(BF16)
        qa_ref[:, (2 * i + 1) * LANES:(2 * i + 2) * LANES] = jnp.where(
            low, pltpu.roll(slab, HEAD_DIM, axis=1), 0.0).astype(BF16)

    ka = head_norm(proj(A_Q, A_Q + A_KV), bd_ref[0:LANES, 0:LANES], gk_ref[...])
    ka = ka * ca + _rotate_pairs(ka, HEAD_DIM // 4) * sa
    ka_ref[:, 0:LANES] = jnp.where(low, ka, 0.0).astype(BF16)
    ka_ref[:, LANES:2 * LANES] = jnp.where(low, pltpu.roll(ka, HEAD_DIM, axis=1), 0.0).astype(BF16)

    va = proj(A_Q + A_KV, A_Q + 2 * A_KV)
    one_col = jnp.where(lane == HEAD_DIM, 1.0, 0.0)
    va_ref[:, 0:LANES] = jnp.where(low, va, one_col).astype(BF16)
    va_ref[:, LANES:2 * LANES] = jnp.where(low, pltpu.roll(va, HEAD_DIM, axis=1), one_col).astype(BF16)

    o = A_Q + 2 * A_KV
    qb = proj(o, o + B_QK)
    qb = (qb * _tile4(cb) + _rotate_pairs(qb, HEAD_DIM // 2) * _tile4(sb)) * Q_SCALE
    qb_ref[...] = qb.astype(BF16)
    kb = proj(o + B_QK, o + 2 * B_QK)
    kb = kb * _tile4(cb) + _rotate_pairs(kb, HEAD_DIM // 2) * _tile4(sb)
    kb_ref[...] = kb.astype(BF16)
    vb = proj(o + 2 * B_QK, o + 2 * B_QK + B_V)
    first_col = jnp.where(lane == 0, 1.0, 0.0).astype(BF16)
    for hd in range(N_HEADS_B):
        vb_ref[:, (2 * hd) * LANES:(2 * hd + 1) * LANES] = vb[:, hd * LANES:(hd + 1) * LANES].astype(BF16)
        vb_ref[:, (2 * hd + 1) * LANES:(2 * hd + 2) * LANES] = first_col


def _inproj(h2d, g, w_bf, gq, gk, tabs, bd, seq, tm):
    n, d = h2d.shape
    nt = seq // tm
    ca, sa, cb, sb = tabs
    row = lambda i: (i, 0)
    const = lambda i: (0, 0)
    tab = lambda i: (i % nt, 0)
    widths = (N_HEADS_A * LANES, N_KV_A * LANES, N_KV_A * LANES, B_QK, B_QK, 2 * B_V)
    return pl.pallas_call(
        _inproj_kernel,
        grid=(n // tm,),
        in_specs=[pl.BlockSpec((tm, d), row), pl.BlockSpec((1, d), const),
                  pl.BlockSpec(w_bf.shape, const), pl.BlockSpec((1, A_Q), const),
                  pl.BlockSpec((1, A_KV), const)]
                 + [pl.BlockSpec((tm, LANES), tab)] * 4
                 + [pl.BlockSpec(bd.shape, const)],
        out_specs=[pl.BlockSpec((tm, w), row) for w in widths],
        out_shape=[jax.ShapeDtypeStruct((n, w), BF16) for w in widths],
        compiler_params=_cparams(("parallel",)),
        name="inproj",
    )(h2d, g, w_bf, gq, gk, ca, sa, cb, sb, bd)


def _flash(qs, k_ref, v_ref, tk):
    nk = k_ref.shape[0] // tk

    def scores(q, c):
        return lax.dot_general(q, k_ref[c * tk:(c + 1) * tk, :], (((1,), (1,)), ((), ())),
                               preferred_element_type=F32)

    ms = [jnp.full((q.shape[0], 1), -jnp.inf, F32) for q in qs]
    accs = [jnp.zeros((q.shape[0], v_ref.shape[1]), F32) for q in qs]
    ss = [scores(q, 0) for q in qs]
    for c in range(nk):
        nxt = [scores(q, c + 1) if c + 1 < nk else None for q in qs]
        for t, s in enumerate(ss):
            m_new = jnp.maximum(ms[t], jnp.max(s, axis=-1, keepdims=True))
            p = jnp.exp2((s - m_new).astype(BF16))
            accs[t] = jnp.exp2(ms[t] - m_new) * accs[t] + jnp.dot(
                p, v_ref[c * tk:(c + 1) * tk, :], preferred_element_type=F32)
            ms[t] = m_new
        ss = nxt
    return accs


def _attn_a_kernel(q_ref, k_ref, v_ref, o_ref, *, tq, tk):
    g = N_HEADS_A // N_KV_A
    tiles = range(q_ref.shape[0] // tq)
    qs = [jnp.concatenate([q_ref[t * tq:(t + 1) * tq, j * LANES:(j + 1) * LANES] for j in range(g)],
                          axis=0) for t in tiles]
    low = _lane((tq, LANES)) < HEAD_DIM
    for t, acc in zip(tiles, _flash(qs, k_ref, v_ref, tk)):
        o = acc * (1.0 / acc[:, HEAD_DIM:HEAD_DIM + 1])
        slabs = [jnp.where(low, o[(2 * i) * tq:(2 * i + 1) * tq],
                           pltpu.roll(o[(2 * i + 1) * tq:(2 * i + 2) * tq], HEAD_DIM, axis=1))
                 for i in range(g // 2)]
        o_ref[t * tq:(t + 1) * tq, :] = jnp.concatenate(slabs, axis=1).astype(o_ref.dtype)


def _attn_a(qa, ka, va, tq, tk):
    b, s, _ = qa.shape
    g = N_HEADS_A // N_KV_A
    rows = tq * Q_TILES
    return pl.pallas_call(
        functools.partial(_attn_a_kernel, tq=tq, tk=tk),
        grid=(b, N_KV_A, s // rows),
        in_specs=[pl.BlockSpec((None, rows, g * LANES), lambda bi, h, i: (bi, i, h)),
                  pl.BlockSpec((None, s, LANES), lambda bi, h, i: (bi, 0, h)),
                  pl.BlockSpec((None, s, LANES), lambda bi, h, i: (bi, 0, h))],
        out_specs=pl.BlockSpec((None, rows, g * HEAD_DIM), lambda bi, h, i: (bi, i, h)),
        out_shape=jax.ShapeDtypeStruct((b, s, A_Q), BF16),
        compiler_params=_cparams(("parallel", "parallel", "parallel")),
        name="attn_a",
    )(qa, ka, va)


def _attn_b_kernel(lam_ref, gs_ref, q_ref, k_ref, v_ref, o_ref, *, tq, tk, lam_init):
    tiles = range(q_ref.shape[0] // tq)
    low = _lane((tq, LANES)) < HEAD_DIM
    zero = jnp.zeros((tq, LANES), q_ref.dtype)
    qs = []
    for t in tiles:
        q = q_ref[t * tq:(t + 1) * tq, :]
        qs.append(jnp.concatenate([jnp.where(low, q, zero), jnp.where(low, zero, q)], axis=0))
    lp = lam_ref[...]
    lam = (jnp.exp(jnp.sum(lp[0:1] * lp[1:2], axis=-1, keepdims=True))
           - jnp.exp(jnp.sum(lp[2:3] * lp[3:4], axis=-1, keepdims=True)) + lam_init)
    w = 2 * HEAD_DIM
    for t, acc in zip(tiles, _flash(qs, k_ref, v_ref, tk)):
        o = acc[:, 0:w] * (1.0 / acc[:, w:w + 1])
        d = o[0:tq] - lam * o[tq:2 * tq]
        o_ref[t * tq:(t + 1) * tq, :] = (_rms(d, gs_ref[...]) * (1.0 - lam_init)).astype(o_ref.dtype)


def _attn_b(lam_p, gs, qb, kb, vb, tq, tk, lam_init):
    b, s, _ = qb.shape
    rows = tq * Q_TILES
    return pl.pallas_call(
        functools.partial(_attn_b_kernel, tq=tq, tk=tk, lam_init=lam_init),
        grid=(b, N_HEADS_B, s // rows),
        in_specs=[pl.BlockSpec(lam_p.shape, lambda bi, h, i: (0, 0)),
                  pl.BlockSpec(gs.shape, lambda bi, h, i: (0, 0)),
                  pl.BlockSpec((None, rows, LANES), lambda bi, h, i: (bi, i, h)),
                  pl.BlockSpec((None, s, LANES), lambda bi, h, i: (bi, 0, h)),
                  pl.BlockSpec((None, s, 2 * LANES), lambda bi, h, i: (bi, 0, h))],
        out_specs=pl.BlockSpec((None, rows, LANES), lambda bi, h, i: (bi, i, h)),
        out_shape=jax.ShapeDtypeStruct((b, s, B_V), BF16),
        compiler_params=_cparams(("parallel", "parallel", "parallel")),
        name="attn_b",
    )(lam_p, gs, qb, kb, vb)


def _outproj_kernel(oa_ref, ob_ref, h_ref, w_ref, ga_ref, gpost_ref, gpre_ref, h2_ref, hn_ref):
    a = _rms(oa_ref[...].astype(F32), ga_ref[...]).astype(BF16)
    mix = (jnp.dot(a, w_ref[0:A_Q, :], preferred_element_type=F32)
           + jnp.dot(ob_ref[...], w_ref[A_Q:, :], preferred_element_type=F32))
    h2 = h_ref[...] + _rms(mix, gpost_ref[...])
    h2_ref[...] = h2
    hn = _rms(h2, gpre_ref[...])
    if hn_ref.shape[1] == LANES:
        _store_row_tiles(hn_ref, hn)
    else:
        hn_ref[...] = hn.astype(hn_ref.dtype)


def _outproj(oa, ob, h2d, w_bf, ga, gpost, gpre, row_tiles, tm):
    n, d = h2d.shape
    row = lambda i: (i, 0)
    const = lambda i: (0, 0)
    if row_tiles:
        hn_spec = pl.BlockSpec((tm * SUBLANES, LANES), row)
        hn_shape = jax.ShapeDtypeStruct((n * SUBLANES, LANES), F32)
    else:
        hn_spec = pl.BlockSpec((tm, d), row)
        hn_shape = jax.ShapeDtypeStruct((n, d), BF16)
    return pl.pallas_call(
        _outproj_kernel,
        grid=(n // tm,),
        in_specs=[pl.BlockSpec((tm, A_Q), row), pl.BlockSpec((tm, B_V), row),
                  pl.BlockSpec((tm, d), row), pl.BlockSpec(w_bf.shape, const),
                  pl.BlockSpec((1, A_Q), const), pl.BlockSpec((1, d), const),
                  pl.BlockSpec((1, d), const)],
        out_specs=[pl.BlockSpec((tm, d), row), hn_spec],
        out_shape=[jax.ShapeDtypeStruct((n, d), F32), hn_shape],
        compiler_params=_cparams(("parallel",)),
        name="outproj",
    )(oa, ob, h2d, w_bf, ga, gpost, gpre)


def _swiglu_accumulate(x_ref, wg_ref, wu_ref, wd_ref, acc_ref):
    gate = jnp.dot(x_ref[...], wg_ref[...], preferred_element_type=F32)
    up = jnp.dot(x_ref[...], wu_ref[...], preferred_element_type=F32)
    act = (gate * jax.nn.sigmoid(gate) * up).astype(BF16)
    acc_ref[...] += jnp.dot(act, wd_ref[...], preferred_element_type=F32)


def _ffn_dense_kernel(x_ref, wg_ref, wu_ref, wd_ref, h_ref, g_ref, o_ref, acc_ref):
    @pl.when(pl.program_id(1) == 0)
    def _():
        acc_ref[...] = jnp.zeros_like(acc_ref)

    _swiglu_accumulate(x_ref, wg_ref, wu_ref, wd_ref, acc_ref)

    @pl.when(pl.program_id(1) == pl.num_programs(1) - 1)
    def _():
        o_ref[...] = h_ref[...] + _rms(acc_ref[...], g_ref[...])


def _ffn_dense(x_bf, wg, wu, wd, h2d, g, tm, tf):
    n, d = h2d.shape
    f = wg.shape[1]
    return pl.pallas_call(
        _ffn_dense_kernel,
        grid=(n // tm, f // tf),
        in_specs=[pl.BlockSpec((tm, d), lambda i, j: (i, 0)),
                  pl.BlockSpec((d, tf), lambda i, j: (0, j)),
                  pl.BlockSpec((d, tf), lambda i, j: (0, j)),
                  pl.BlockSpec((tf, d), lambda i, j: (j, 0)),
                  pl.BlockSpec((tm, d), lambda i, j: (i, 0)),
                  pl.BlockSpec((1, d), lambda i, j: (0, 0))],
        out_specs=pl.BlockSpec((tm, d), lambda i, j: (i, 0)),
        out_shape=jax.ShapeDtypeStruct((n, d), F32),
        scratch_shapes=[pltpu.VMEM((tm, d), F32)],
        compiler_params=_cparams(("parallel", "arbitrary")),
        name="ffn_dense",
    )(x_bf, wg, wu, wd, h2d, g)


def _ffn_moe_kernel(be_ref, idx_ref, idx_next_ref, src_hbm, wg_ref, wu_ref, wd_ref, o_ref,
                    rows_ref, x_ref, acc_ref, sem, *, n_ff):
    del be_ref
    i, j = pl.program_id(0), pl.program_id(1)
    nb = pl.num_programs(0)
    slot = i % 2
    tb = x_ref.shape[0]
    share = rows_ref.shape[1] // (SUBLANES * n_ff)

    @pl.when(jnp.logical_and(i == 0, j == 0))
    def _():
        _start_rows(idx_ref, src_hbm, rows_ref.at[0], sem.at[0])

    @pl.when(j == 0)
    def _():
        _wait_rows(src_hbm, rows_ref.at[slot], sem.at[slot])
        x_ref[...] = _load_row_tiles(rows_ref.at[slot].at[0:tb * SUBLANES]).astype(x_ref.dtype)
        acc_ref[...] = jnp.zeros_like(acc_ref)

    _start_row_range(idx_next_ref, src_hbm, rows_ref.at[1 - slot], sem.at[1 - slot], j * share, share)
    _swiglu_accumulate(x_ref, wg_ref, wu_ref, wd_ref, acc_ref)

    @pl.when(j == n_ff - 1)
    def _():
        _store_row_tiles(o_ref, acc_ref[...])

    @pl.when(jnp.logical_and(i == nb - 1, j == n_ff - 1))
    def _():
        _wait_rows(src_hbm, rows_ref.at[1 - slot], sem.at[1 - slot])


def _ffn_moe(block_expert, slot_src, src, wg, wu, wd, tb, tf):
    cap = slot_src.shape[0]
    d, f = wg.shape[1], wg.shape[2]
    nb, n_ff = cap // tb, f // tf
    ring_rows = n_ff * (-(-tb // (n_ff * ROW_UNROLL)) * ROW_UNROLL)
    idx3 = jnp.pad(slot_src.reshape(nb, tb), ((0, 0), (0, ring_rows - tb))).reshape(nb, 1, ring_rows)
    grid_spec = pltpu.PrefetchScalarGridSpec(
        num_scalar_prefetch=1,
        grid=(nb, n_ff),
        in_specs=[pl.BlockSpec((1, 1, ring_rows), lambda i, j, be: (i, 0, 0), memory_space=pltpu.SMEM),
                  pl.BlockSpec((1, 1, ring_rows), lambda i, j, be: (jnp.minimum(i + 1, nb - 1), 0, 0),
                               memory_space=pltpu.SMEM),
                  pl.BlockSpec(memory_space=pl.ANY),
                  pl.BlockSpec((None, d, tf), lambda i, j, be: (be[i], 0, j)),
                  pl.BlockSpec((None, d, tf), lambda i, j, be: (be[i], 0, j)),
                  pl.BlockSpec((None, tf, d), lambda i, j, be: (be[i], j, 0))],
        out_specs=pl.BlockSpec((tb * SUBLANES, LANES), lambda i, j, be: (i, 0)),
        scratch_shapes=[pltpu.VMEM((2, ring_rows * SUBLANES, LANES), src.dtype),
                        pltpu.VMEM((tb, d), BF16),
                        pltpu.VMEM((tb, d), F32),
                        pltpu.SemaphoreType.DMA((2,))],
    )
    return pl.pallas_call(
        functools.partial(_ffn_moe_kernel, n_ff=n_ff),
        grid_spec=grid_spec,
        out_shape=jax.ShapeDtypeStruct((cap * SUBLANES, LANES), F32),
        compiler_params=_cparams(("arbitrary", "arbitrary")),
        name="ffn_moe",
    )(block_expert, idx3, idx3, src, wg, wu, wd)


def _route_kernel(x_ref, wr_ref, tri_ref, oi_ref, ow_ref, cnt_ref, carry_ref):
    @pl.when(pl.program_id(0) == 0)
    def _():
        carry_ref[...] = jnp.zeros_like(carry_ref)

    logits = lax.dot_general(wr_ref[...], _load_row_tiles(x_ref), (((1,), (1,)), ((), ())),
                             precision=lax.Precision.HIGHEST, preferred_element_type=F32)
    shape = logits.shape
    eidx = lax.broadcasted_iota(jnp.int32, shape, 0)
    m1 = jnp.max(logits, axis=0, keepdims=True)
    i1 = jnp.min(jnp.where(logits == m1, eidx, N_EXPERTS), axis=0, keepdims=True)
    oh1 = eidx == i1
    rest = jnp.where(oh1, -jnp.inf, logits)
    m2 = jnp.max(rest, axis=0, keepdims=True)
    i2 = jnp.min(jnp.where(rest == m2, eidx, N_EXPERTS), axis=0, keepdims=True)
    oh2 = eidx == i2
    e = jnp.exp(m2 - m1)
    w1 = 1.0 / (1.0 + e)
    w2 = e * w1

    cnt = jnp.where(oh1, 1.0, jnp.where(oh2, 1.0, 0.0))
    incl = jnp.dot(cnt.astype(BF16), tri_ref[...], preferred_element_type=F32)
    excl = incl - cnt + carry_ref[:, 0:1]
    r1 = jnp.sum(jnp.where(oh1, excl, 0.0), axis=0, keepdims=True).astype(jnp.int32)
    r2 = jnp.sum(jnp.where(oh2, excl, 0.0), axis=0, keepdims=True).astype(jnp.int32)
    carry_ref[...] = carry_ref[...] + jnp.sum(cnt, axis=1, keepdims=True)

    bc = lambda v: jnp.broadcast_to(v, shape)
    oi_ref[...] = jnp.where(eidx == 0, bc(i1), jnp.where(eidx == 1, bc(i2),
                            jnp.where(eidx == 2, bc(r1), bc(r2))))
    ow_ref[...] = jnp.where(eidx == 0, bc(w1), bc(w2))
    cnt_ref[...] = carry_ref[...].astype(jnp.int32)


def _route(hn_tiles, wr_t, tr):
    n = hn_tiles.shape[0] // SUBLANES
    tri = (lax.broadcasted_iota(jnp.int32, (tr, tr), 0)
           <= lax.broadcasted_iota(jnp.int32, (tr, tr), 1)).astype(BF16)
    return pl.pallas_call(
        _route_kernel,
        grid=(n // tr,),
        in_specs=[pl.BlockSpec((tr * SUBLANES, LANES), lambda i: (i, 0)),
                  pl.BlockSpec(wr_t.shape, lambda i: (0, 0)),
                  pl.BlockSpec((tr, tr), lambda i: (0, 0))],
        out_specs=[pl.BlockSpec((N_EXPERTS, tr), lambda i: (0, i)),
                   pl.BlockSpec((N_EXPERTS, tr), lambda i: (0, i)),
                   pl.BlockSpec((N_EXPERTS, LANES), lambda i: (0, 0))],
        out_shape=[jax.ShapeDtypeStruct((N_EXPERTS, n), jnp.int32),
                   jax.ShapeDtypeStruct((N_EXPERTS, n), F32),
                   jax.ShapeDtypeStruct((N_EXPERTS, LANES), jnp.int32)],
        scratch_shapes=[pltpu.VMEM((N_EXPERTS, LANES), F32)],
        compiler_params=_cparams(("arbitrary",)),
        name="route",
    )(hn_tiles, wr_t, tri)


def _row_copy(src_hbm, dst_ref, sem, src_sublane, dst_sublane):
    return pltpu.make_async_copy(src_hbm.at[pl.ds(pl.multiple_of(src_sublane, SUBLANES), SUBLANES), :],
                                 dst_ref.at[pl.ds(pl.multiple_of(dst_sublane, SUBLANES), SUBLANES), :],
                                 sem)


def _start_rows(idx_ref, src_hbm, dst_ref, sem):
    rows = dst_ref.shape[0] // SUBLANES

    def group(g, c):
        base = g * ROW_UNROLL
        for u in range(ROW_UNROLL):
            _row_copy(src_hbm, dst_ref, sem, idx_ref[0, 0, base + u],
                      (base + u) * SUBLANES).start(priority=u % 2)
        return c

    lax.fori_loop(0, rows // ROW_UNROLL, group, 0)


def _start_row_range(idx_ref, src_hbm, dst_ref, sem, base, count):
    for u in range(count):
        _row_copy(src_hbm, dst_ref, sem, idx_ref[0, 0, base + u],
                  (base + u) * SUBLANES).start(priority=u % 2)


def _wait_rows(src_hbm, dst_ref, sem):
    for r in range(dst_ref.shape[0] // SUBLANES):
        _row_copy(src_hbm, dst_ref, sem, 0, r * SUBLANES).wait()


def _ring_step(start, wait):
    i = pl.program_id(0)
    slot = i % 2

    @pl.when(i == 0)
    def _():
        start(0, False)

    @pl.when(i + 1 < pl.num_programs(0))
    def _():
        start(1 - slot, True)

    wait(slot)
    return slot


def _next_block(nb):
    return lambda i: (jnp.minimum(i + 1, nb - 1), 0, 0)


def _combine_kernel(d1_ref, d2_ref, d1n_ref, d2n_ref, w1_ref, w2_ref, h_ref, g_ref, y_hbm, o_ref,
                    y1_ref, y2_ref, sem):
    def start(slot, nxt):
        _start_rows(d1n_ref if nxt else d1_ref, y_hbm, y1_ref.at[slot], sem.at[0, slot])
        _start_rows(d2n_ref if nxt else d2_ref, y_hbm, y2_ref.at[slot], sem.at[1, slot])

    def wait(slot):
        _wait_rows(y_hbm, y1_ref.at[slot], sem.at[0, slot])
        _wait_rows(y_hbm, y2_ref.at[slot], sem.at[1, slot])

    slot = _ring_step(start, wait)
    f = (w1_ref[...] * _load_row_tiles(y1_ref.at[slot])
         + w2_ref[...] * _load_row_tiles(y2_ref.at[slot]))
    o_ref[...] = h_ref[...] + _rms(f, g_ref[...])


def _combine(dest1, dest2, w1, w2, h2d, g, y, tc):
    n, d = h2d.shape
    nb = n // tc
    idx = lambda v: v.reshape(nb, 1, tc)
    smem = pl.BlockSpec((1, 1, tc), lambda i: (i, 0, 0), memory_space=pltpu.SMEM)
    smem_next = pl.BlockSpec((1, 1, tc), _next_block(nb), memory_space=pltpu.SMEM)
    return pl.pallas_call(
        _combine_kernel,
        grid=(nb,),
        in_specs=[smem, smem, smem_next, smem_next,
                  pl.BlockSpec((tc, 1), lambda i: (i, 0)), pl.BlockSpec((tc, 1), lambda i: (i, 0)),
                  pl.BlockSpec((tc, d), lambda i: (i, 0)), pl.BlockSpec((1, d), lambda i: (0, 0)),
                  pl.BlockSpec(memory_space=pl.ANY)],
        out_specs=pl.BlockSpec((tc, d), lambda i: (i, 0)),
        out_shape=jax.ShapeDtypeStruct((n, d), F32),
        scratch_shapes=[pltpu.VMEM((2, tc * SUBLANES, LANES), y.dtype),
                        pltpu.VMEM((2, tc * SUBLANES, LANES), y.dtype),
                        pltpu.SemaphoreType.DMA((2, 2))],
        compiler_params=_cparams(("arbitrary",)),
        name="moe_combine",
    )(idx(dest1), idx(dest2), idx(dest1), idx(dest2), w1, w2, h2d, g, y)


def _moe(hn_tiles, h2d, w_router, wg, wu, wd, g_post, tiles):
    n = hn_tiles.shape[0] // SUBLANES
    tb = tiles["moe_rows"]
    oi, ow, cnt = _route(hn_tiles, w_router.T, tiles["route"])
    e1, e2, r1, r2 = oi[0], oi[1], oi[2], oi[3]
    counts = cnt[:, 0]
    padded = ((counts + tb - 1) // tb) * tb
    pad_end = jnp.cumsum(padded)
    pad_start = pad_end - padded
    dest1 = pad_start[e1] + r1
    dest2 = pad_start[e2] + r2
    n_blocks = -(-(2 * n) // tb) + N_EXPERTS
    tok = jnp.arange(n, dtype=jnp.int32) * SUBLANES
    slot_src = jnp.zeros((n_blocks * tb,), jnp.int32).at[jnp.concatenate([dest1, dest2])].set(
        jnp.concatenate([tok, tok]))
    block_start = jnp.arange(n_blocks, dtype=jnp.int32) * tb
    block_expert = jnp.minimum(jnp.searchsorted(pad_end, block_start, side="right"),
                               N_EXPERTS - 1).astype(jnp.int32)
    y = _ffn_moe(block_expert, slot_src, hn_tiles, wg, wu, wd, tb, tiles["ff"])
    return _combine(dest1 * SUBLANES, dest2 * SUBLANES, ow[0].reshape(n, 1), ow[1].reshape(n, 1),
                    h2d, g_post, y, tiles["combine"])


def _rope_tables(seq):
    t = jnp.arange(seq, dtype=jnp.int32)

    def cs(pos, dim):
        inv = ROPE_THETA ** (-jnp.arange(0, dim, 2, dtype=F32) / dim)
        ang = pos.astype(F32)[:, None] * inv[None, :]
        return jnp.cos(ang), jnp.sin(ang)

    rc, rs = cs(t // GRID_W, HEAD_DIM // 2)
    cc, cs_ = cs(t % GRID_W, HEAD_DIM // 2)
    sc, ss = cs(t, HEAD_DIM)
    cos_a = jnp.concatenate([rc, rc, cc, cc] * 2, axis=1)
    sin_a = jnp.concatenate([-rs, rs, -cs_, cs_] * 2, axis=1)
    cos_b = jnp.concatenate([sc, sc] * 2, axis=1)
    sin_b = jnp.concatenate([-ss, ss] * 2, axis=1)
    return cos_a, sin_a, cos_b, sin_b


def _tiles(n, seq, d_ff):
    pick = lambda total, want: math.gcd(total, want)
    return {
        "rows": pick(seq, 512),
        "tq_a": pick(seq // Q_TILES, 128), "tq_b": pick(seq // Q_TILES, 256), "tk": pick(seq, 512),
        "ffn_rows": pick(n, 1024), "ff": pick(d_ff, 512),
        "moe_rows": 1024 if n >= 4096 else 128, "route": pick(n, 512),
        "combine": pick(n, 256),
    }


def kernel(x, w_in, w_out, g_pre_mix, g_post_mix, g_pre_ffn, g_post_ffn, g_qnorm_a, g_knorm_a, g_out_a, diff_lambda, g_subln_b, w_gate_dense, w_up_dense, w_down_dense, w_router, w_gate_moe, w_up_moe, w_down_moe):
    b, s, d = x.shape
    assert d == D_MODEL, "row-tile layout needs one activation row per (8, 128) tile"
    n = b * s
    depth = w_in.shape[0]
    tiles = _tiles(n, s, w_gate_dense.shape[-1])
    tabs = _rope_tables(s)
    bd = (lax.broadcasted_iota(jnp.int32, (A_Q, A_Q), 0) // HEAD_DIM
          == lax.broadcasted_iota(jnp.int32, (A_Q, A_Q), 1) // HEAD_DIM).astype(BF16)
    row = lambda v: v.reshape(1, -1).astype(F32)

    h = x.reshape(n, d)
    for i in range(depth):
        gq = row(jnp.tile(g_qnorm_a[i], N_HEADS_A)) * Q_SCALE
        gk = row(jnp.tile(g_knorm_a[i], N_KV_A))
        qa, ka, va, qb, kb, vb = _inproj(h, row(g_pre_mix[i]), w_in[i].astype(BF16), gq, gk, tabs, bd,
                                         s, tiles["rows"])
        shp = lambda v: v.reshape(b, s, v.shape[-1])
        oa = _attn_a(shp(qa), shp(ka), shp(va), tiles["tq_a"], tiles["tk"])
        lam_init = 0.8 - 0.6 * math.exp(-0.3 * i)
        ob = _attn_b(diff_lambda[i].astype(F32), row(g_subln_b[i]), shp(qb), shp(kb), shp(vb),
                     tiles["tq_b"], tiles["tk"], lam_init)
        moe_layer = i % 2 == 1
        h, hn = _outproj(oa.reshape(n, A_Q), ob.reshape(n, B_V), h, w_out[i].astype(BF16),
                         row(g_out_a[i]), row(g_post_mix[i]), row(g_pre_ffn[i]),
                         moe_layer, tiles["rows"])
        j = i // 2
        if moe_layer:
            h = _moe(hn, h, w_router[j], w_gate_moe[j].astype(BF16), w_up_moe[j].astype(BF16),
                     w_down_moe[j].astype(BF16), row(g_post_ffn[i]), tiles)
        else:
            h = _ffn_dense(hn, w_gate_dense[j].astype(BF16), w_up_dense[j].astype(BF16),
                           w_down_dense[j].astype(BF16), h, row(g_post_ffn[i]),
                           tiles["ffn_rows"], tiles["ff"])
    return h.reshape(b, s, d)
```

```python
import functools
import math

import jax
import jax.numpy as jnp
from jax import lax
from jax.experimental import pallas as pl
from jax.experimental.pallas import tpu as pltpu

F32 = jnp.float32
BF16 = jnp.bfloat16

HEAD_DIM = 64
N_HEADS_A = 8
N_KV_A = 2
N_HEADS_B = 4
A_Q = N_HEADS_A * HEAD_DIM
A_KV = N_KV_A * HEAD_DIM
B_QK = N_HEADS_B * 2 * HEAD_DIM
B_V = N_HEADS_B * 2 * HEAD_DIM
GRID_W = 64
ROPE_THETA = 10000.0
EPS = 1e-6
N_EXPERTS = 8
LOG2E = 1.4426950408889634
LANES = 128
SUBLANES = 8
D_MODEL = SUBLANES * LANES
Q_SCALE = (HEAD_DIM ** -0.5) * LOG2E

VMEM_LIMIT = 56 * 1024 * 1024
ROW_UNROLL = 8
Q_TILES = 2


def _cparams(sem):
    return pltpu.CompilerParams(dimension_semantics=sem, vmem_limit_bytes=VMEM_LIMIT)


def _rms(x, g):
    return x * lax.rsqrt(jnp.mean(x * x, axis=-1, keepdims=True) + EPS) * g


def _lane(shape):
    return lax.broadcasted_iota(jnp.int32, shape, len(shape) - 1)


def _tile4(t):
    return jnp.concatenate([t, t, t, t], axis=1)


def _store_row_tiles(ref, x):
    rows = x.shape[0]
    for c in range(SUBLANES):
        ref[pl.ds(c, rows, stride=SUBLANES), :] = x[:, c * LANES:(c + 1) * LANES]


def _load_row_tiles(ref):
    rows = ref.shape[0] // SUBLANES
    return jnp.concatenate([ref[pl.ds(c, rows, stride=SUBLANES), :] for c in range(SUBLANES)], axis=1)


def _rotate_pairs(x, half):
    w = x.shape[-1]
    first = (_lane(x.shape) % (2 * half)) < half
    return jnp.where(first, pltpu.roll(x, w - half, axis=1), pltpu.roll(x, half, axis=1))


def _inproj_kernel(h_ref, g_ref, w_ref, gq_ref, gk_ref, ca_ref, sa_ref, cb_ref, sb_ref, bd_ref,
                   qa_ref, ka_ref, va_ref, qb_ref, kb_ref, vb_ref):
    hn = _rms(h_ref[...], g_ref[...]).astype(BF16)
    tm = hn.shape[0]
    lane = _lane((tm, LANES))
    low = lane < HEAD_DIM
    ca, sa = ca_ref[...], sa_ref[...]
    cb, sb = cb_ref[...], sb_ref[...]

    def proj(lo, hi):
        return jnp.dot(hn, w_ref[:, lo:hi], preferred_element_type=F32)

    def head_norm(x, bd, g):
        ms = jnp.dot((x * x).astype(BF16), bd, preferred_element_type=F32) * (1.0 / HEAD_DIM)
        return x * lax.rsqrt(ms + EPS) * g

    qa = head_norm(proj(0, A_Q), bd_ref[...], gq_ref[...])
    qa = qa * _tile4(ca) + _rotate_pairs(qa, HEAD_DIM // 4) * _tile4(sa)
    for i in range(N_HEADS_A // 2):
        slab = qa[:, i * LANES:(i + 1) * LANES]
        qa_ref[:, (2 * i) * LANES:(2 * i + 1) * LANES] = jnp.where(low, slab, 0.0).astype(BF16)
        qa_ref[:, (2 * i + 1) * LANES:(2 * i + 2) * LANES] = jnp.where(
            low, pltpu.roll(slab, HEAD_DIM, axis=1), 0.0).astype(BF16)

    ka = head_norm(proj(A_Q, A_Q + A_KV), bd_ref[0:LANES, 0:LANES], gk_ref[...])
    ka = ka * ca + _rotate_pairs(ka, HEAD_DIM // 4) * sa
    ka_ref[:, 0:LANES] = jnp.where(low, ka, 0.0).astype(BF16)
    ka_ref[:, LANES:2 * LANES] = jnp.where(low, pltpu.roll(ka, HEAD_DIM, axis=1), 0.0).astype(BF16)

    va = proj(A_Q + A_KV, A_Q + 2 * A_KV)
    one_col = jnp.where(lane == HEAD_DIM, 1.0, 0.0)
    va_ref[:, 0:LANES] = jnp.where(low, va, one_col).astype(BF16)
    va_ref[:, LANES:2 * LANES] = jnp.where(low, pltpu.roll(va, HEAD_DIM, axis=1), one_col).astype(BF16)

    o = A_Q + 2 * A_KV
    qb = proj(o, o + B_QK)
    qb = (qb * _tile4(cb) + _rotate_pairs(qb, HEAD_DIM // 2) * _tile4(sb)) * Q_SCALE
    qb_ref[...] = qb.astype(BF16)
    kb = proj(o + B_QK, o + 2 * B_QK)
    kb = kb * _tile4(cb) + _rotate_pairs(kb, HEAD_DIM // 2) * _tile4(sb)
    kb_ref[...] = kb.astype(BF16)
    vb = proj(o + 2 * B_QK, o + 2 * B_QK + B_V)
    first_col = jnp.where(lane == 0, 1.0, 0.0).astype(BF16)
    for hd in range(N_HEADS_B):
        vb_ref[:, (2 * hd) * LANES:(2 * hd + 1) * LANES] = vb[:, hd * LANES:(hd + 1) * LANES].astype(BF16)
        vb_ref[:, (2 * hd + 1) * LANES:(2 * hd + 2) * LANES] = first_col


def _inproj(h2d, g, w_bf, gq, gk, tabs, bd, seq, tm):
    n, d = h2d.shape
    nt = seq // tm
    ca, sa, cb, sb = tabs
    row = lambda i: (i, 0)
    const = lambda i: (0, 0)
    tab = lambda i: (i % nt, 0)
    widths = (N_HEADS_A * LANES, N_KV_A * LANES, N_KV_A * LANES, B_QK, B_QK, 2 * B_V)
    return pl.pallas_call(
        _inproj_kernel,
        grid=(n // tm,),
        in_specs=[pl.BlockSpec((tm, d), row), pl.BlockSpec((1, d), const),
                  pl.BlockSpec(w_bf.shape, const), pl.BlockSpec((1, A_Q), const),
                  pl.BlockSpec((1, A_KV), const)]
                 + [pl.BlockSpec((tm, LANES), tab)] * 4
                 + [pl.BlockSpec(bd.shape, const)],
        out_specs=[pl.BlockSpec((tm, w), row) for w in widths],
        out_shape=[jax.ShapeDtypeStruct((n, w), BF16) for w in widths],
        compiler_params=_cparams(("parallel",)),
        name="inproj",
    )(h2d, g, w_bf, gq, gk, ca, sa, cb, sb, bd)


def _flash(qs, k_ref, v_ref, tk):
    nk = k_ref.shape[0] // tk

    def scores(q, c):
        return lax.dot_general(q, k_ref[c * tk:(c + 1) * tk, :], (((1,), (1,)), ((), ())),
                               preferred_element_type=F32)

    ms = [jnp.full((q.shape[0], 1), -jnp.inf, F32) for q in qs]
    accs = [jnp.zeros((q.shape[0], v_ref.shape[1]), F32) for q in qs]
    ss = [scores(q, 0) for q in qs]
    for c in range(nk):
        nxt = [scores(q, c + 1) if c + 1 < nk else None for q in qs]
        for t, s in enumerate(ss):
            m_new = jnp.maximum(ms[t], jnp.max(s, axis=-1, keepdims=True))
            p = jnp.exp2((s - m_new).astype(BF16))
            accs[t] = jnp.exp2(ms[t] - m_new) * accs[t] + jnp.dot(
                p, v_ref[c * tk:(c + 1) * tk, :], preferred_element_type=F32)
            ms[t] = m_new
        ss = nxt
    return accs


def _attn_a_kernel(q_ref, k_ref, v_ref, o_ref, *, tq, tk):
    g = N_HEADS_A // N_KV_A
    tiles = range(q_ref.shape[0] // tq)
    qs = [jnp.concatenate([q_ref[t * tq:(t + 1) * tq, j * LANES:(j + 1) * LANES] for j in range(g)],
                          axis=0) for t in tiles]
    low = _lane((tq, LANES)) < HEAD_DIM
    for t, acc in zip(tiles, _flash(qs, k_ref, v_ref, tk)):
        o = acc * (1.0 / acc[:, HEAD_DIM:HEAD_DIM + 1])
        slabs = [jnp.where(low, o[(2 * i) * tq:(2 * i + 1) * tq],
                           pltpu.roll(o[(2 * i + 1) * tq:(2 * i + 2) * tq], HEAD_DIM, axis=1))
                 for i in range(g // 2)]
        o_ref[t * tq:(t + 1) * tq, :] = jnp.concatenate(slabs, axis=1).astype(o_ref.dtype)


def _attn_a(qa, ka, va, tq, tk):
    b, s, _ = qa.shape
    g = N_HEADS_A // N_KV_A
    rows = tq * Q_TILES
    return pl.pallas_call(
        functools.partial(_attn_a_kernel, tq=tq, tk=tk),
        grid=(b, N_KV_A, s // rows),
        in_specs=[pl.BlockSpec((None, rows, g * LANES), lambda bi, h, i: (bi, i, h)),
                  pl.BlockSpec((None, s, LANES), lambda bi, h, i: (bi, 0, h)),
                  pl.BlockSpec((None, s, LANES), lambda bi, h, i: (bi, 0, h))],
        out_specs=pl.BlockSpec((None, rows, g * HEAD_DIM), lambda bi, h, i: (bi, i, h)),
        out_shape=jax.ShapeDtypeStruct((b, s, A_Q), BF16),
        compiler_params=_cparams(("parallel", "parallel", "parallel")),
        name="attn_a",
    )(qa, ka, va)


def _attn_b_kernel(lam_ref, gs_ref, q_ref, k_ref, v_ref, o_ref, *, tq, tk, lam_init):
    tiles = range(q_ref.shape[0] // tq)
    low = _lane((tq, LANES)) < HEAD_DIM
    zero = jnp.zeros((tq, LANES), q_ref.dtype)
    qs = []
    for t in tiles:
        q = q_ref[t * tq:(t + 1) * tq, :]
        qs.append(jnp.concatenate([jnp.where(low, q, zero), jnp.where(low, zero, q)], axis=0))
    lp = lam_ref[...]
    lam = (jnp.exp(jnp.sum(lp[0:1] * lp[1:2], axis=-1, keepdims=True))
           - jnp.exp(jnp.sum(lp[2:3] * lp[3:4], axis=-1, keepdims=True)) + lam_init)
    w = 2 * HEAD_DIM
    for t, acc in zip(tiles, _flash(qs, k_ref, v_ref, tk)):
        o = acc[:, 0:w] * (1.0 / acc[:, w:w + 1])
        d = o[0:tq] - lam * o[tq:2 * tq]
        o_ref[t * tq:(t + 1) * tq, :] = (_rms(d, gs_ref[...]) * (1.0 - lam_init)).astype(o_ref.dtype)


def _attn_b(lam_p, gs, qb, kb, vb, tq, tk, lam_init):
    b, s, _ = qb.shape
    rows = tq * Q_TILES
    return pl.pallas_call(
        functools.partial(_attn_b_kernel, tq=tq, tk=tk, lam_init=lam_init),
        grid=(b, N_HEADS_B, s // rows),
        in_specs=[pl.BlockSpec(lam_p.shape, lambda bi, h, i: (0, 0)),
                  pl.BlockSpec(gs.shape, lambda bi, h, i: (0, 0)),
                  pl.BlockSpec((None, rows, LANES), lambda bi, h, i: (bi, i, h)),
                  pl.BlockSpec((None, s, LANES), lambda bi, h, i: (bi, 0, h)),
                  pl.BlockSpec((None, s, 2 * LANES), lambda bi, h, i: (bi, 0, h))],
        out_specs=pl.BlockSpec((None, rows, LANES), lambda bi, h, i: (bi, i, h)),
        out_shape=jax.ShapeDtypeStruct((b, s, B_V), BF16),
        compiler_params=_cparams(("parallel", "parallel", "parallel")),
        name="attn_b",
    )(lam_p, gs, qb, kb, vb)


def _outproj_kernel(oa_ref, ob_ref, h_ref, w_ref, ga_ref, gpost_ref, gpre_ref, h2_ref, hn_ref):
    a = _rms(oa_ref[...].astype(F32), ga_ref[...]).astype(BF16)
    mix = (jnp.dot(a, w_ref[0:A_Q, :], preferred_element_type=F32)
           + jnp.dot(ob_ref[...], w_ref[A_Q:, :], preferred_element_type=F32))
    h2 = h_ref[...] + _rms(mix, gpost_ref[...])
    h2_ref[...] = h2
    hn = _rms(h2, gpre_ref[...])
    if hn_ref.shape[1] == LANES:
        _store_row_tiles(hn_ref, hn)
    else:
        hn_ref[...] = hn.astype(hn_ref.dtype)


def _outproj(oa, ob, h2d, w_bf, ga, gpost, gpre, row_tiles, tm):
    n, d = h2d.shape
    row = lambda i: (i, 0)
    const = lambda i: (0, 0)
    if row_tiles:
        hn_spec = pl.BlockSpec((tm * SUBLANES, LANES), row)
        hn_shape = jax.ShapeDtypeStruct((n * SUBLANES, LANES), F32)
    else:
        hn_spec = pl.BlockSpec((tm, d), row)
        hn_shape = jax.ShapeDtypeStruct((n, d), BF16)
    return pl.pallas_call(
        _outproj_kernel,
        grid=(n // tm,),
        in_specs=[pl.BlockSpec((tm, A_Q), row), pl.BlockSpec((tm, B_V), row),
                  pl.BlockSpec((tm, d), row), pl.BlockSpec(w_bf.shape, const),
                  pl.BlockSpec((1, A_Q), const), pl.BlockSpec((1, d), const),
                  pl.BlockSpec((1, d), const)],
        out_specs=[pl.BlockSpec((tm, d), row), hn_spec],
        out_shape=[jax.ShapeDtypeStruct((n, d), F32), hn_shape],
        compiler_params=_cparams(("parallel",)),
        name="outproj",
    )(oa, ob, h2d, w_bf, ga, gpost, gpre)


def _swiglu_accumulate(x_ref, wg_ref, wu_ref, wd_ref, acc_ref):
    gate = jnp.dot(x_ref[...], wg_ref[...], preferred_element_type=F32)
    up = jnp.dot(x_ref[...], wu_ref[...], preferred_element_type=F32)
    act = (gate * jax.nn.sigmoid(gate) * up).astype(BF16)
    acc_ref[...] += jnp.dot(act, wd_ref[...], preferred_element_type=F32)


def _ffn_dense_kernel(x_ref, wg_ref, wu_ref, wd_ref, h_ref, g_ref, o_ref, acc_ref):
    @pl.when(pl.program_id(1) == 0)
    def _():
        acc_ref[...] = jnp.zeros_like(acc_ref)

    _swiglu_accumulate(x_ref, wg_ref, wu_ref, wd_ref, acc_ref)

    @pl.when(pl.program_id(1) == pl.num_programs(1) - 1)
    def _():
        o_ref[...] = h_ref[...] + _rms(acc_ref[...], g_ref[...])


def _ffn_dense(x_bf, wg, wu, wd, h2d, g, tm, tf):
    n, d = h2d.shape
    f = wg.shape[1]
    return pl.pallas_call(
        _ffn_dense_kernel,
        grid=(n // tm, f // tf),
        in_specs=[pl.BlockSpec((tm, d), lambda i, j: (i, 0)),
                  pl.BlockSpec((d, tf), lambda i, j: (0, j)),
                  pl.BlockSpec((d, tf), lambda i, j: (0, j)),
                  pl.BlockSpec((tf, d), lambda i, j: (j, 0)),
                  pl.BlockSpec((tm, d), lambda i, j: (i, 0)),
                  pl.BlockSpec((1, d), lambda i, j: (0, 0))],
        out_specs=pl.BlockSpec((tm, d), lambda i, j: (i, 0)),
        out_shape=jax.ShapeDtypeStruct((n, d), F32),
        scratch_shapes=[pltpu.VMEM((tm, d), F32)],
        compiler_params=_cparams(("parallel", "arbitrary")),
        name="ffn_dense",
    )(x_bf, wg, wu, wd, h2d, g)


def _ffn_moe_kernel(be_ref, idx_ref, idx_next_ref, src_hbm, wg_ref, wu_ref, wd_ref, o_ref,
                    rows_ref, x_ref, acc_ref, sem, *, n_ff):
    del be_ref
    i, j = pl.program_id(0), pl.program_id(1)
    nb = pl.num_programs(0)
    slot = i % 2
    tb = x_ref.shape[0]
    share = rows_ref.shape[1] // (SUBLANES * n_ff)

    @pl.when(jnp.logical_and(i == 0, j == 0))
    def _():
        _start_rows(idx_ref, src_hbm, rows_ref.at[0], sem.at[0])

    @pl.when(j == 0)
    def _():
        _wait_rows(src_hbm, rows_ref.at[slot], sem.at[slot])
        x_ref[...] = _load_row_tiles(rows_ref.at[slot].at[0:tb * SUBLANES]).astype(x_ref.dtype)
        acc_ref[...] = jnp.zeros_like(acc_ref)

    _start_row_range(idx_next_ref, src_hbm, rows_ref.at[1 - slot], sem.at[1 - slot], j * share, share)
    _swiglu_accumulate(x_ref, wg_ref, wu_ref, wd_ref, acc_ref)

    @pl.when(j == n_ff - 1)
    def _():
        _store_row_tiles(o_ref, acc_ref[...])

    @pl.when(jnp.logical_and(i == nb - 1, j == n_ff - 1))
    def _():
        _wait_rows(src_hbm, rows_ref.at[1 - slot], sem.at[1 - slot])


def _ffn_moe(block_expert, slot_src, src, wg, wu, wd, tb, tf):
    cap = slot_src.shape[0]
    d, f = wg.shape[1], wg.shape[2]
    nb, n_ff = cap // tb, f // tf
    ring_rows = n_ff * (-(-tb // (n_ff * ROW_UNROLL)) * ROW_UNROLL)
    idx3 = jnp.pad(slot_src.reshape(nb, tb), ((0, 0), (0, ring_rows - tb))).reshape(nb, 1, ring_rows)
    grid_spec = pltpu.PrefetchScalarGridSpec(
        num_scalar_prefetch=1,
        grid=(nb, n_ff),
        in_specs=[pl.BlockSpec((1, 1, ring_rows), lambda i, j, be: (i, 0, 0), memory_space=pltpu.SMEM),
                  pl.BlockSpec((1, 1, ring_rows), lambda i, j, be: (jnp.minimum(i + 1, nb - 1), 0, 0),
                               memory_space=pltpu.SMEM),
                  pl.BlockSpec(memory_space=pl.ANY),
                  pl.BlockSpec((None, d, tf), lambda i, j, be: (be[i], 0, j)),
                  pl.BlockSpec((None, d, tf), lambda i, j, be: (be[i], 0, j)),
                  pl.BlockSpec((None, tf, d), lambda i, j, be: (be[i], j, 0))],
        out_specs=pl.BlockSpec((tb * SUBLANES, LANES), lambda i, j, be: (i, 0)),
        scratch_shapes=[pltpu.VMEM((2, ring_rows * SUBLANES, LANES), src.dtype),
                        pltpu.VMEM((tb, d), BF16),
                        pltpu.VMEM((tb, d), F32),
                        pltpu.SemaphoreType.DMA((2,))],
    )
    return pl.pallas_call(
        functools.partial(_ffn_moe_kernel, n_ff=n_ff),
        grid_spec=grid_spec,
        out_shape=jax.ShapeDtypeStruct((cap * SUBLANES, LANES), F32),
        compiler_params=_cparams(("arbitrary", "arbitrary")),
        name="ffn_moe",
    )(block_expert, idx3, idx3, src, wg, wu, wd)


def _route_kernel(x_ref, wr_ref, tri_ref, oi_ref, ow_ref, cnt_ref, carry_ref):
    @pl.when(pl.program_id(0) == 0)
    def _():
        carry_ref[...] = jnp.zeros_like(carry_ref)

    logits = lax.dot_general(wr_ref[...], _load_row_tiles(x_ref), (((1,), (1,)), ((), ())),
                             precision=lax.Precision.HIGHEST, preferred_element_type=F32)
    shape = logits.shape
    eidx = lax.broadcasted_iota(jnp.int32, shape, 0)
    m1 = jnp.max(logits, axis=0, keepdims=True)
    i1 = jnp.min(jnp.where(logits == m1, eidx, N_EXPERTS), axis=0, keepdims=True)
    oh1 = eidx == i1
    rest = jnp.where(oh1, -jnp.inf, logits)
    m2 = jnp.max(rest, axis=0, keepdims=True)
    i2 = jnp.min(jnp.where(rest == m2, eidx, N_EXPERTS), axis=0, keepdims=True)
    oh2 = eidx == i2
    e = jnp.exp(m2 - m1)
    w1 = 1.0 / (1.0 + e)
    w2 = e * w1

    cnt = jnp.where(oh1, 1.0, jnp.where(oh2, 1.0, 0.0))
    incl = jnp.dot(cnt.astype(BF16), tri_ref[...], preferred_element_type=F32)
    excl = incl - cnt + carry_ref[:, 0:1]
    r1 = jnp.sum(jnp.where(oh1, excl, 0.0), axis=0, keepdims=True).astype(jnp.int32)
    r2 = jnp.sum(jnp.where(oh2, excl, 0.0), axis=0, keepdims=True).astype(jnp.int32)
    carry_ref[...] = carry_ref[...] + jnp.sum(cnt, axis=1, keepdims=True)

    bc = lambda v: jnp.broadcast_to(v, shape)
    oi_ref[...] = jnp.where(eidx == 0, bc(i1), jnp.where(eidx == 1, bc(i2),
                            jnp.where(eidx == 2, bc(r1), bc(r2))))
    ow_ref[...] = jnp.where(eidx == 0, bc(w1), bc(w2))
    cnt_ref[...] = carry_ref[...].astype(jnp.int32)


def _route(hn_tiles, wr_t, tr):
    n = hn_tiles.shape[0] // SUBLANES
    tri = (lax.broadcasted_iota(jnp.int32, (tr, tr), 0)
           <= lax.broadcasted_iota(jnp.int32, (tr, tr), 1)).astype(BF16)
    return pl.pallas_call(
        _route_kernel,
        grid=(n // tr,),
        in_specs=[pl.BlockSpec((tr * SUBLANES, LANES), lambda i: (i, 0)),
                  pl.BlockSpec(wr_t.shape, lambda i: (0, 0)),
                  pl.BlockSpec((tr, tr), lambda i: (0, 0))],
        out_specs=[pl.BlockSpec((N_EXPERTS, tr), lambda i: (0, i)),
                   pl.BlockSpec((N_EXPERTS, tr), lambda i: (0, i)),
                   pl.BlockSpec((N_EXPERTS, LANES), lambda i: (0, 0))],
        out_shape=[jax.ShapeDtypeStruct((N_EXPERTS, n), jnp.int32),
                   jax.ShapeDtypeStruct((N_EXPERTS, n), F32),
                   jax.ShapeDtypeStruct((N_EXPERTS, LANES), jnp.int32)],
        scratch_shapes=[pltpu.VMEM((N_EXPERTS, LANES), F32)],
        compiler_params=_cparams(("arbitrary",)),
        name="route",
    )(hn_tiles, wr_t, tri)


def _row_copy(src_hbm, dst_ref, sem, src_sublane, dst_sublane):
    return pltpu.make_async_copy(src_hbm.at[pl.ds(pl.multiple_of(src_sublane, SUBLANES), SUBLANES), :],
                                 dst_ref.at[pl.ds(pl.multiple_of(dst_sublane, SUBLANES), SUBLANES), :],
                                 sem)


def _start_rows(idx_ref, src_hbm, dst_ref, sem):
    rows = dst_ref.shape[0] // SUBLANES

    def group(g, c):
        base = g * ROW_UNROLL
        for u in range(ROW_UNROLL):
            _row_copy(src_hbm, dst_ref, sem, idx_ref[0, 0, base + u],
                      (base + u) * SUBLANES).start(priority=u % 2)
        return c

    lax.fori_loop(0, rows // ROW_UNROLL, group, 0)


def _start_row_range(idx_ref, src_hbm, dst_ref, sem, base, count):
    for u in range(count):
        _row_copy(src_hbm, dst_ref, sem, idx_ref[0, 0, base + u], (base + u) * SUBLANES).start()


def _wait_rows(src_hbm, dst_ref, sem):
    for r in range(dst_ref.shape[0] // SUBLANES):
        _row_copy(src_hbm, dst_ref, sem, 0, r * SUBLANES).wait()


def _ring_step(start, wait):
    i = pl.program_id(0)
    slot = i % 2

    @pl.when(i == 0)
    def _():
        start(0, False)

    @pl.when(i + 1 < pl.num_programs(0))
    def _():
        start(1 - slot, True)

    wait(slot)
    return slot


def _next_block(nb):
    return lambda i: (jnp.minimum(i + 1, nb - 1), 0, 0)


def _combine_kernel(d1_ref, d2_ref, d1n_ref, d2n_ref, w1_ref, w2_ref, h_ref, g_ref, y_hbm, o_ref,
                    y1_ref, y2_ref, sem):
    def start(slot, nxt):
        _start_rows(d1n_ref if nxt else d1_ref, y_hbm, y1_ref.at[slot], sem.at[0, slot])
        _start_rows(d2n_ref if nxt else d2_ref, y_hbm, y2_ref.at[slot], sem.at[1, slot])

    def wait(slot):
        _wait_rows(y_hbm, y1_ref.at[slot], sem.at[0, slot])
        _wait_rows(y_hbm, y2_ref.at[slot], sem.at[1, slot])

    slot = _ring_step(start, wait)
    f = (w1_ref[...] * _load_row_tiles(y1_ref.at[slot])
         + w2_ref[...] * _load_row_tiles(y2_ref.at[slot]))
    o_ref[...] = h_ref[...] + _rms(f, g_ref[...])


def _combine(dest1, dest2, w1, w2, h2d, g, y, tc):
    n, d = h2d.shape
    nb = n // tc
    idx = lambda v: v.reshape(nb, 1, tc)
    smem = pl.BlockSpec((1, 1, tc), lambda i: (i, 0, 0), memory_space=pltpu.SMEM)
    smem_next = pl.BlockSpec((1, 1, tc), _next_block(nb), memory_space=pltpu.SMEM)
    return pl.pallas_call(
        _combine_kernel,
        grid=(nb,),
        in_specs=[smem, smem, smem_next, smem_next,
                  pl.BlockSpec((tc, 1), lambda i: (i, 0)), pl.BlockSpec((tc, 1), lambda i: (i, 0)),
                  pl.BlockSpec((tc, d), lambda i: (i, 0)), pl.BlockSpec((1, d), lambda i: (0, 0)),
                  pl.BlockSpec(memory_space=pl.ANY)],
        out_specs=pl.BlockSpec((tc, d), lambda i: (i, 0)),
        out_shape=jax.ShapeDtypeStruct((n, d), F32),
        scratch_shapes=[pltpu.VMEM((2, tc * SUBLANES, LANES), y.dtype),
                        pltpu.VMEM((2, tc * SUBLANES, LANES), y.dtype),
                        pltpu.SemaphoreType.DMA((2, 2))],
        compiler_params=_cparams(("arbitrary",)),
        name="moe_combine",
    )(idx(dest1), idx(dest2), idx(dest1), idx(dest2), w1, w2, h2d, g, y)


def _moe(hn_tiles, h2d, w_router, wg, wu, wd, g_post, tiles):
    n = hn_tiles.shape[0] // SUBLANES
    tb = tiles["moe_rows"]
    oi, ow, cnt = _route(hn_tiles, w_router.T, tiles["route"])
    e1, e2, r1, r2 = oi[0], oi[1], oi[2], oi[3]
    counts = cnt[:, 0]
    padded = ((counts + tb - 1) // tb) * tb
    pad_end = jnp.cumsum(padded)
    pad_start = pad_end - padded
    dest1 = pad_start[e1] + r1
    dest2 = pad_start[e2] + r2
    n_blocks = -(-(2 * n) // tb) + N_EXPERTS
    tok = jnp.arange(n, dtype=jnp.int32) * SUBLANES
    slot_src = jnp.zeros((n_blocks * tb,), jnp.int32).at[jnp.concatenate([dest1, dest2])].set(
        jnp.concatenate([tok, tok]))
    block_start = jnp.arange(n_blocks, dtype=jnp.int32) * tb
    block_expert = jnp.minimum(jnp.searchsorted(pad_end, block_start, side="right"),
                               N_EXPERTS - 1).astype(jnp.int32)
    y = _ffn_moe(block_expert, slot_src, hn_tiles, wg, wu, wd, tb, tiles["ff"])
    return _combine(dest1 * SUBLANES, dest2 * SUBLANES, ow[0].reshape(n, 1), ow[1].reshape(n, 1),
                    h2d, g_post, y, tiles["combine"])


def _rope_tables(seq):
    t = jnp.arange(seq, dtype=jnp.int32)

    def cs(pos, dim):
        inv = ROPE_THETA ** (-jnp.arange(0, dim, 2, dtype=F32) / dim)
        ang = pos.astype(F32)[:, None] * inv[None, :]
        return jnp.cos(ang), jnp.sin(ang)

    rc, rs = cs(t // GRID_W, HEAD_DIM // 2)
    cc, cs_ = cs(t % GRID_W, HEAD_DIM // 2)
    sc, ss = cs(t, HEAD_DIM)
    cos_a = jnp.concatenate([rc, rc, cc, cc] * 2, axis=1)
    sin_a = jnp.concatenate([-rs, rs, -cs_, cs_] * 2, axis=1)
    cos_b = jnp.concatenate([sc, sc] * 2, axis=1)
    sin_b = jnp.concatenate([-ss, ss] * 2, axis=1)
    return cos_a, sin_a, cos_b, sin_b


def _tiles(n, seq, d_ff):
    pick = lambda total, want: math.gcd(total, want)
    return {
        "rows": pick(seq, 512),
        "tq_a": pick(seq // Q_TILES, 128), "tq_b": pick(seq // Q_TILES, 256), "tk": pick(seq, 512),
        "ffn_rows": pick(n, 1024), "ff": pick(d_ff, 512),
        "moe_rows": 1024 if n >= 4096 else 128, "route": pick(n, 512),
        "combine": pick(n, 256),
    }


def kernel(x, w_in, w_out, g_pre_mix, g_post_mix, g_pre_ffn, g_post_ffn, g_qnorm_a, g_knorm_a, g_out_a, diff_lambda, g_subln_b, w_gate_dense, w_up_dense, w_down_dense, w_router, w_gate_moe, w_up_moe, w_down_moe):
    b, s, d = x.shape
    assert d == D_MODEL, "row-tile layout needs one activation row per (8, 128) tile"
    n = b * s
    depth = w_in.shape[0]
    tiles = _tiles(n, s, w_gate_dense.shape[-1])
    tabs = _rope_tables(s)
    bd = (lax.broadcasted_iota(jnp.int32, (A_Q, A_Q), 0) // HEAD_DIM
          == lax.broadcasted_iota(jnp.int32, (A_Q, A_Q), 1) // HEAD_DIM).astype(BF16)
    row = lambda v: v.reshape(1, -1).astype(F32)

    h = x.reshape(n, d)
    for i in range(depth):
        gq = row(jnp.tile(g_qnorm_a[i], N_HEADS_A)) * Q_SCALE
        gk = row(jnp.tile(g_knorm_a[i], N_KV_A))
        qa, ka, va, qb, kb, vb = _inproj(h, row(g_pre_mix[i]), w_in[i].astype(BF16), gq, gk, tabs, bd,
                                         s, tiles["rows"])
        shp = lambda v: v.reshape(b, s, v.shape[-1])
        oa = _attn_a(shp(qa), shp(ka), shp(va), tiles["tq_a"], tiles["tk"])
        lam_init = 0.8 - 0.6 * math.exp(-0.3 * i)
        ob = _attn_b(diff_lambda[i].astype(F32), row(g_subln_b[i]), shp(qb), shp(kb), shp(vb),
                     tiles["tq_b"], tiles["tk"], lam_init)
        moe_layer = i % 2 == 1
        h, hn = _outproj(oa.reshape(n, A_Q), ob.reshape(n, B_V), h, w_out[i].astype(BF16),
                         row(g_out_a[i]), row(g_post_mix[i]), row(g_pre_ffn[i]),
                         moe_layer, tiles["rows"])
        j = i // 2
        if moe_layer:
            h = _moe(hn, h, w_router[j], w_gate_moe[j].astype(BF16), w_up_moe[j].astype(BF16),
                     w_down_moe[j].astype(BF16), row(g_post_ffn[i]), tiles)
        else:
            h = _ffn_dense(hn, w_gate_dense[j].astype(BF16), w_up_dense[j].astype(BF16),
                           w_down_dense[j].astype(BF16), h, row(g_post_ffn[i]),
                           tiles["ffn_rows"], tiles["ff"])
    return h.reshape(b, s, d)
```

```python
import functools
import math

import jax
import jax.numpy as jnp
from jax import lax
from jax.experimental import pallas as pl
from jax.experimental.pallas import tpu as pltpu

F32 = jnp.float32
BF16 = jnp.bfloat16

HEAD_DIM = 64
N_HEADS_A = 8
N_KV_A = 2
N_HEADS_B = 4
A_Q = N_HEADS_A * HEAD_DIM
A_KV = N_KV_A * HEAD_DIM
B_QK = N_HEADS_B * 2 * HEAD_DIM
B_V = N_HEADS_B * 2 * HEAD_DIM
GRID_W = 64
ROPE_THETA = 10000.0
EPS = 1e-6
N_EXPERTS = 8
LOG2E = 1.4426950408889634
LANES = 128
SUBLANES = 8
D_MODEL = SUBLANES * LANES
Q_SCALE = (HEAD_DIM ** -0.5) * LOG2E

VMEM_LIMIT = 56 * 1024 * 1024
ROW_UNROLL = 8
Q_TILES = 2


def _cparams(sem):
    return pltpu.CompilerParams(dimension_semantics=sem, vmem_limit_bytes=VMEM_LIMIT)


def _rms(x, g):
    return x * lax.rsqrt(jnp.mean(x * x, axis=-1, keepdims=True) + EPS) * g


def _lane(shape):
    return lax.broadcasted_iota(jnp.int32, shape, len(shape) - 1)


def _tile4(t):
    return jnp.concatenate([t, t, t, t], axis=1)


def _store_row_tiles(ref, x):
    rows = x.shape[0]
    for c in range(SUBLANES):
        ref[pl.ds(c, rows, stride=SUBLANES), :] = x[:, c * LANES:(c + 1) * LANES]


def _load_row_tiles(ref):
    rows = ref.shape[0] // SUBLANES
    return jnp.concatenate([ref[pl.ds(c, rows, stride=SUBLANES), :] for c in range(SUBLANES)], axis=1)


def _rotate_pairs(x, half):
    w = x.shape[-1]
    first = (_lane(x.shape) % (2 * half)) < half
    return jnp.where(first, pltpu.roll(x, w - half, axis=1), pltpu.roll(x, half, axis=1))


def _inproj_kernel(h_ref, g_ref, w_ref, gq_ref, gk_ref, ca_ref, sa_ref, cb_ref, sb_ref, bd_ref,
                   qa_ref, ka_ref, va_ref, qb_ref, kb_ref, vb_ref):
    hn = _rms(h_ref[...], g_ref[...]).astype(BF16)
    tm = hn.shape[0]
    lane = _lane((tm, LANES))
    low = lane < HEAD_DIM
    ca, sa = ca_ref[...], sa_ref[...]
    cb, sb = cb_ref[...], sb_ref[...]

    def proj(lo, hi):
        return jnp.dot(hn, w_ref[:, lo:hi], preferred_element_type=F32)

    def head_norm(x, bd, g):
        ms = jnp.dot((x * x).astype(BF16), bd, preferred_element_type=F32) * (1.0 / HEAD_DIM)
        return x * lax.rsqrt(ms + EPS) * g

    qa = head_norm(proj(0, A_Q), bd_ref[...], gq_ref[...])
    qa = qa * _tile4(ca) + _rotate_pairs(qa, HEAD_DIM // 4) * _tile4(sa)
    for i in range(N_HEADS_A // 2):
        slab = qa[:, i * LANES:(i + 1) * LANES]
        qa_ref[:, (2 * i) * LANES:(2 * i + 1) * LANES] = jnp.where(low, slab, 0.0).astype(BF16)
        qa_ref[:, (2 * i + 1) * LANES:(2 * i + 2) * LANES] = jnp.where(
            low, pltpu.roll(slab, HEAD_DIM, axis=1), 0.0).astype(BF16)

    ka = head_norm(proj(A_Q, A_Q + A_KV), bd_ref[0:LANES, 0:LANES], gk_ref[...])
    ka = ka * ca + _rotate_pairs(ka, HEAD_DIM // 4) * sa
    ka_ref[:, 0:LANES] = jnp.where(low, ka, 0.0).astype(BF16)
    ka_ref[:, LANES:2 * LANES] = jnp.where(low, pltpu.roll(ka, HEAD_DIM, axis=1), 0.0).astype(BF16)

    va = proj(A_Q + A_KV, A_Q + 2 * A_KV)
    one_col = jnp.where(lane == HEAD_DIM, 1.0, 0.0)
    va_ref[:, 0:LANES] = jnp.where(low, va, one_col).astype(BF16)
    va_ref[:, LANES:2 * LANES] = jnp.where(low, pltpu.roll(va, HEAD_DIM, axis=1), one_col).astype(BF16)

    o = A_Q + 2 * A_KV
    qb = proj(o, o + B_QK)
    qb = (qb * _tile4(cb) + _rotate_pairs(qb, HEAD_DIM // 2) * _tile4(sb)) * Q_SCALE
    qb_ref[...] = qb.astype(BF16)
    kb = proj(o + B_QK, o + 2 * B_QK)
    kb = kb * _tile4(cb) + _rotate_pairs(kb, HEAD_DIM // 2) * _tile4(sb)
    kb_ref[...] = kb.astype(BF16)
    vb = proj(o + 2 * B_QK, o + 2 * B_QK + B_V)
    first_col = jnp.where(lane == 0, 1.0, 0.0).astype(BF16)
    for hd in range(N_HEADS_B):
        vb_ref[:, (2 * hd) * LANES:(2 * hd + 1) * LANES] = vb[:, hd * LANES:(hd + 1) * LANES].astype(BF16)
        vb_ref[:, (2 * hd + 1) * LANES:(2 * hd + 2) * LANES] = first_col


def _inproj(h2d, g, w_bf, gq, gk, tabs, bd, seq, tm):
    n, d = h2d.shape
    nt = seq // tm
    ca, sa, cb, sb = tabs
    row = lambda i: (i, 0)
    const = lambda i: (0, 0)
    tab = lambda i: (i % nt, 0)
    widths = (N_HEADS_A * LANES, N_KV_A * LANES, N_KV_A * LANES, B_QK, B_QK, 2 * B_V)
    return pl.pallas_call(
        _inproj_kernel,
        grid=(n // tm,),
        in_specs=[pl.BlockSpec((tm, d), row), pl.BlockSpec((1, d), const),
                  pl.BlockSpec(w_bf.shape, const), pl.BlockSpec((1, A_Q), const),
                  pl.BlockSpec((1, A_KV), const)]
                 + [pl.BlockSpec((tm, LANES), tab)] * 4
                 + [pl.BlockSpec(bd.shape, const)],
        out_specs=[pl.BlockSpec((tm, w), row) for w in widths],
        out_shape=[jax.ShapeDtypeStruct((n, w), BF16) for w in widths],
        compiler_params=_cparams(("parallel",)),
        name="inproj",
    )(h2d, g, w_bf, gq, gk, ca, sa, cb, sb, bd)


def _flash(qs, k_ref, v_ref, tk):
    nk = k_ref.shape[0] // tk

    def scores(q, c):
        return lax.dot_general(q, k_ref[c * tk:(c + 1) * tk, :], (((1,), (1,)), ((), ())),
                               preferred_element_type=F32)

    ms = [jnp.full((q.shape[0], 1), -jnp.inf, F32) for q in qs]
    accs = [jnp.zeros((q.shape[0], v_ref.shape[1]), F32) for q in qs]
    ss = [scores(q, 0) for q in qs]
    for c in range(nk):
        nxt = [scores(q, c + 1) if c + 1 < nk else None for q in qs]
        for t, s in enumerate(ss):
            m_new = jnp.maximum(ms[t], jnp.max(s, axis=-1, keepdims=True))
            p = jnp.exp2((s - m_new).astype(BF16))
            accs[t] = jnp.exp2(ms[t] - m_new) * accs[t] + jnp.dot(
                p, v_ref[c * tk:(c + 1) * tk, :], preferred_element_type=F32)
            ms[t] = m_new
        ss = nxt
    return accs


def _attn_a_kernel(q_ref, k_ref, v_ref, o_ref, *, tq, tk):
    g = N_HEADS_A // N_KV_A
    tiles = range(q_ref.shape[0] // tq)
    qs = [jnp.concatenate([q_ref[t * tq:(t + 1) * tq, j * LANES:(j + 1) * LANES] for j in range(g)],
                          axis=0) for t in tiles]
    low = _lane((tq, LANES)) < HEAD_DIM
    for t, acc in zip(tiles, _flash(qs, k_ref, v_ref, tk)):
        o = acc * (1.0 / acc[:, HEAD_DIM:HEAD_DIM + 1])
        slabs = [jnp.where(low, o[(2 * i) * tq:(2 * i + 1) * tq],
                           pltpu.roll(o[(2 * i + 1) * tq:(2 * i + 2) * tq], HEAD_DIM, axis=1))
                 for i in range(g // 2)]
        o_ref[t * tq:(t + 1) * tq, :] = jnp.concatenate(slabs, axis=1).astype(o_ref.dtype)


def _attn_a(qa, ka, va, tq, tk):
    b, s, _ = qa.shape
    g = N_HEADS_A // N_KV_A
    rows = tq * Q_TILES
    return pl.pallas_call(
        functools.partial(_attn_a_kernel, tq=tq, tk=tk),
        grid=(b, N_KV_A, s // rows),
        in_specs=[pl.BlockSpec((None, rows, g * LANES), lambda bi, h, i: (bi, i, h)),
                  pl.BlockSpec((None, s, LANES), lambda bi, h, i: (bi, 0, h)),
                  pl.BlockSpec((None, s, LANES), lambda bi, h, i: (bi, 0, h))],
        out_specs=pl.BlockSpec((None, rows, g * HEAD_DIM), lambda bi, h, i: (bi, i, h)),
        out_shape=jax.ShapeDtypeStruct((b, s, A_Q), BF16),
        compiler_params=_cparams(("parallel", "parallel", "parallel")),
        name="attn_a",
    )(qa, ka, va)


def _attn_b_kernel(lam_ref, gs_ref, q_ref, k_ref, v_ref, o_ref, *, tq, tk, lam_init):
    tiles = range(q_ref.shape[0] // tq)
    low = _lane((tq, LANES)) < HEAD_DIM
    zero = jnp.zeros((tq, LANES), q_ref.dtype)
    qs = []
    for t in tiles:
        q = q_ref[t * tq:(t + 1) * tq, :]
        qs.append(jnp.concatenate([jnp.where(low, q, zero), jnp.where(low, zero, q)], axis=0))
    lp = lam_ref[...]
    lam = (jnp.exp(jnp.sum(lp[0:1] * lp[1:2], axis=-1, keepdims=True))
           - jnp.exp(jnp.sum(lp[2:3] * lp[3:4], axis=-1, keepdims=True)) + lam_init)
    w = 2 * HEAD_DIM
    for t, acc in zip(tiles, _flash(qs, k_ref, v_ref, tk)):
        o = acc[:, 0:w] * (1.0 / acc[:, w:w + 1])
        d = o[0:tq] - lam * o[tq:2 * tq]
        o_ref[t * tq:(t + 1) * tq, :] = (_rms(d, gs_ref[...]) * (1.0 - lam_init)).astype(o_ref.dtype)


def _attn_b(lam_p, gs, qb, kb, vb, tq, tk, lam_init):
    b, s, _ = qb.shape
    rows = tq * Q_TILES
    return pl.pallas_call(
        functools.partial(_attn_b_kernel, tq=tq, tk=tk, lam_init=lam_init),
        grid=(b, N_HEADS_B, s // rows),
        in_specs=[pl.BlockSpec(lam_p.shape, lambda bi, h, i: (0, 0)),
                  pl.BlockSpec(gs.shape, lambda bi, h, i: (0, 0)),
                  pl.BlockSpec((None, rows, LANES), lambda bi, h, i: (bi, i, h)),
                  pl.BlockSpec((None, s, LANES), lambda bi, h, i: (bi, 0, h)),
                  pl.BlockSpec((None, s, 2 * LANES), lambda bi, h, i: (bi, 0, h))],
        out_specs=pl.BlockSpec((None, rows, LANES), lambda bi, h, i: (bi, i, h)),
        out_shape=jax.ShapeDtypeStruct((b, s, B_V), BF16),
        compiler_params=_cparams(("parallel", "parallel", "parallel")),
        name="attn_b",
    )(lam_p, gs, qb, kb, vb)


def _outproj_kernel(oa_ref, ob_ref, h_ref, w_ref, ga_ref, gpost_ref, gpre_ref, h2_ref, hn_ref):
    a = _rms(oa_ref[...].astype(F32), ga_ref[...]).astype(BF16)
    mix = (jnp.dot(a, w_ref[0:A_Q, :], preferred_element_type=F32)
           + jnp.dot(ob_ref[...], w_ref[A_Q:, :], preferred_element_type=F32))
    h2 = h_ref[...] + _rms(mix, gpost_ref[...])
    h2_ref[...] = h2
    hn = _rms(h2, gpre_ref[...])
    if hn_ref.shape[1] == LANES:
        _store_row_tiles(hn_ref, hn)
    else:
        hn_ref[...] = hn.astype(hn_ref.dtype)


def _outproj(oa, ob, h2d, w_bf, ga, gpost, gpre, row_tiles, tm):
    n, d = h2d.shape
    row = lambda i: (i, 0)
    const = lambda i: (0, 0)
    if row_tiles:
        hn_spec = pl.BlockSpec((tm * SUBLANES, LANES), row)
        hn_shape = jax.ShapeDtypeStruct((n * SUBLANES, LANES), F32)
    else:
        hn_spec = pl.BlockSpec((tm, d), row)
        hn_shape = jax.ShapeDtypeStruct((n, d), BF16)
    return pl.pallas_call(
        _outproj_kernel,
        grid=(n // tm,),
        in_specs=[pl.BlockSpec((tm, A_Q), row), pl.BlockSpec((tm, B_V), row),
                  pl.BlockSpec((tm, d), row), pl.BlockSpec(w_bf.shape, const),
                  pl.BlockSpec((1, A_Q), const), pl.BlockSpec((1, d), const),
                  pl.BlockSpec((1, d), const)],
        out_specs=[pl.BlockSpec((tm, d), row), hn_spec],
        out_shape=[jax.ShapeDtypeStruct((n, d), F32), hn_shape],
        compiler_params=_cparams(("parallel",)),
        name="outproj",
    )(oa, ob, h2d, w_bf, ga, gpost, gpre)


def _swiglu_accumulate(x_ref, wg_ref, wu_ref, wd_ref, acc_ref):
    gate = jnp.dot(x_ref[...], wg_ref[...], preferred_element_type=F32)
    up = jnp.dot(x_ref[...], wu_ref[...], preferred_element_type=F32)
    act = (gate * jax.nn.sigmoid(gate) * up).astype(BF16)
    acc_ref[...] += jnp.dot(act, wd_ref[...], preferred_element_type=F32)


def _ffn_dense_kernel(x_ref, wg_ref, wu_ref, wd_ref, h_ref, g_ref, o_ref, acc_ref):
    @pl.when(pl.program_id(1) == 0)
    def _():
        acc_ref[...] = jnp.zeros_like(acc_ref)

    _swiglu_accumulate(x_ref, wg_ref, wu_ref, wd_ref, acc_ref)

    @pl.when(pl.program_id(1) == pl.num_programs(1) - 1)
    def _():
        o_ref[...] = h_ref[...] + _rms(acc_ref[...], g_ref[...])


def _ffn_dense(x_bf, wg, wu, wd, h2d, g, tm, tf):
    n, d = h2d.shape
    f = wg.shape[1]
    return pl.pallas_call(
        _ffn_dense_kernel,
        grid=(n // tm, f // tf),
        in_specs=[pl.BlockSpec((tm, d), lambda i, j: (i, 0)),
                  pl.BlockSpec((d, tf), lambda i, j: (0, j)),
                  pl.BlockSpec((d, tf), lambda i, j: (0, j)),
                  pl.BlockSpec((tf, d), lambda i, j: (j, 0)),
                  pl.BlockSpec((tm, d), lambda i, j: (i, 0)),
                  pl.BlockSpec((1, d), lambda i, j: (0, 0))],
        out_specs=pl.BlockSpec((tm, d), lambda i, j: (i, 0)),
        out_shape=jax.ShapeDtypeStruct((n, d), F32),
        scratch_shapes=[pltpu.VMEM((tm, d), F32)],
        compiler_params=_cparams(("parallel", "arbitrary")),
        name="ffn_dense",
    )(x_bf, wg, wu, wd, h2d, g)


def _ffn_moe_kernel(be_ref, idx_ref, idx_next_ref, src_hbm, wg_ref, wu_ref, wd_ref, o_ref,
                    rows_ref, x_ref, acc_ref, sem, *, n_ff):
    del be_ref
    i, j = pl.program_id(0), pl.program_id(1)
    nb = pl.num_programs(0)
    slot = i % 2
    tb = x_ref.shape[0]
    share = rows_ref.shape[1] // (SUBLANES * n_ff)

    @pl.when(jnp.logical_and(i == 0, j == 0))
    def _():
        _start_rows(idx_ref, src_hbm, rows_ref.at[0], sem.at[0])

    @pl.when(j == 0)
    def _():
        _wait_rows(src_hbm, rows_ref.at[slot], sem.at[slot])
        x_ref[...] = _load_row_tiles(rows_ref.at[slot].at[0:tb * SUBLANES]).astype(x_ref.dtype)
        acc_ref[...] = jnp.zeros_like(acc_ref)

    _start_row_range(idx_next_ref, src_hbm, rows_ref.at[1 - slot], sem.at[1 - slot], j * share, share)
    _swiglu_accumulate(x_ref, wg_ref, wu_ref, wd_ref, acc_ref)

    @pl.when(j == n_ff - 1)
    def _():
        _store_row_tiles(o_ref, acc_ref[...])

    @pl.when(jnp.logical_and(i == nb - 1, j == n_ff - 1))
    def _():
        _wait_rows(src_hbm, rows_ref.at[1 - slot], sem.at[1 - slot])


def _ffn_moe(block_expert, slot_src, src, wg, wu, wd, tb, tf):
    cap = slot_src.shape[0]
    d, f = wg.shape[1], wg.shape[2]
    nb, n_ff = cap // tb, f // tf
    ring_rows = n_ff * (-(-tb // (n_ff * ROW_UNROLL)) * ROW_UNROLL)
    idx3 = jnp.pad(slot_src.reshape(nb, tb), ((0, 0), (0, ring_rows - tb))).reshape(nb, 1, ring_rows)
    grid_spec = pltpu.PrefetchScalarGridSpec(
        num_scalar_prefetch=1,
        grid=(nb, n_ff),
        in_specs=[pl.BlockSpec((1, 1, ring_rows), lambda i, j, be: (i, 0, 0), memory_space=pltpu.SMEM),
                  pl.BlockSpec((1, 1, ring_rows), lambda i, j, be: (jnp.minimum(i + 1, nb - 1), 0, 0),
                               memory_space=pltpu.SMEM),
                  pl.BlockSpec(memory_space=pl.ANY),
                  pl.BlockSpec((None, d, tf), lambda i, j, be: (be[i], 0, j)),
                  pl.BlockSpec((None, d, tf), lambda i, j, be: (be[i], 0, j)),
                  pl.BlockSpec((None, tf, d), lambda i, j, be: (be[i], j, 0))],
        out_specs=pl.BlockSpec((tb * SUBLANES, LANES), lambda i, j, be: (i, 0)),
        scratch_shapes=[pltpu.VMEM((2, ring_rows * SUBLANES, LANES), src.dtype),
                        pltpu.VMEM((tb, d), BF16),
                        pltpu.VMEM((tb, d), F32),
                        pltpu.SemaphoreType.DMA((2,))],
    )
    return pl.pallas_call(
        functools.partial(_ffn_moe_kernel, n_ff=n_ff),
        grid_spec=grid_spec,
        out_shape=jax.ShapeDtypeStruct((cap * SUBLANES, LANES), F32),
        compiler_params=_cparams(("arbitrary", "arbitrary")),
        name="ffn_moe",
    )(block_expert, idx3, idx3, src, wg, wu, wd)


def _route_kernel(x_ref, wr_ref, tri_ref, oi_ref, ow_ref, cnt_ref, carry_ref):
    @pl.when(pl.program_id(0) == 0)
    def _():
        carry_ref[...] = jnp.zeros_like(carry_ref)

    logits = lax.dot_general(wr_ref[...], _load_row_tiles(x_ref), (((1,), (1,)), ((), ())),
                             precision=lax.Precision.HIGHEST, preferred_element_type=F32)
    shape = logits.shape
    eidx = lax.broadcasted_iota(jnp.int32, shape, 0)
    m1 = jnp.max(logits, axis=0, keepdims=True)
    i1 = jnp.min(jnp.where(logits == m1, eidx, N_EXPERTS), axis=0, keepdims=True)
    oh1 = eidx == i1
    rest = jnp.where(oh1, -jnp.inf, logits)
    m2 = jnp.max(rest, axis=0, keepdims=True)
    i2 = jnp.min(jnp.where(rest == m2, eidx, N_EXPERTS), axis=0, keepdims=True)
    oh2 = eidx == i2
    e = jnp.exp(m2 - m1)
    w1 = 1.0 / (1.0 + e)
    w2 = e * w1

    cnt = jnp.where(oh1, 1.0, jnp.where(oh2, 1.0, 0.0))
    incl = jnp.dot(cnt.astype(BF16), tri_ref[...], preferred_element_type=F32)
    excl = incl - cnt + carry_ref[:, 0:1]
    r1 = jnp.sum(jnp.where(oh1, excl, 0.0), axis=0, keepdims=True).astype(jnp.int32)
    r2 = jnp.sum(jnp.where(oh2, excl, 0.0), axis=0, keepdims=True).astype(jnp.int32)
    carry_ref[...] = carry_ref[...] + jnp.sum(cnt, axis=1, keepdims=True)

    bc = lambda v: jnp.broadcast_to(v, shape)
    oi_ref[...] = jnp.where(eidx == 0, bc(i1), jnp.where(eidx == 1, bc(i2),
                            jnp.where(eidx == 2, bc(r1), bc(r2))))
    ow_ref[...] = jnp.where(eidx == 0, bc(w1), bc(w2))
    cnt_ref[...] = carry_ref[...].astype(jnp.int32)


def _route(hn_tiles, wr_t, tr):
    n = hn_tiles.shape[0] // SUBLANES
    tri = (lax.broadcasted_iota(jnp.int32, (tr, tr), 0)
           <= lax.broadcasted_iota(jnp.int32, (tr, tr), 1)).astype(BF16)
    return pl.pallas_call(
        _route_kernel,
        grid=(n // tr,),
        in_specs=[pl.BlockSpec((tr * SUBLANES, LANES), lambda i: (i, 0)),
                  pl.BlockSpec(wr_t.shape, lambda i: (0, 0)),
                  pl.BlockSpec((tr, tr), lambda i: (0, 0))],
        out_specs=[pl.BlockSpec((N_EXPERTS, tr), lambda i: (0, i)),
                   pl.BlockSpec((N_EXPERTS, tr), lambda i: (0, i)),
                   pl.BlockSpec((N_EXPERTS, LANES), lambda i: (0, 0))],
        out_shape=[jax.ShapeDtypeStruct((N_EXPERTS, n), jnp.int32),
                   jax.ShapeDtypeStruct((N_EXPERTS, n), F32),
                   jax.ShapeDtypeStruct((N_EXPERTS, LANES), jnp.int32)],
        scratch_shapes=[pltpu.VMEM((N_EXPERTS, LANES), F32)],
        compiler_params=_cparams(("arbitrary",)),
        name="route",
    )(hn_tiles, wr_t, tri)


def _row_copy(src_hbm, dst_ref, sem, src_sublane, dst_sublane):
    return pltpu.make_async_copy(src_hbm.at[pl.ds(pl.multiple_of(src_sublane, SUBLANES), SUBLANES), :],
                                 dst_ref.at[pl.ds(pl.multiple_of(dst_sublane, SUBLANES), SUBLANES), :],
                                 sem)


def _start_rows(idx_ref, src_hbm, dst_ref, sem):
    rows = dst_ref.shape[0] // SUBLANES

    def group(g, c):
        base = g * ROW_UNROLL
        for u in range(ROW_UNROLL):
            _row_copy(src_hbm, dst_ref, sem, idx_ref[0, 0, base + u],
                      (base + u) * SUBLANES).start(priority=u % 2)
        return c

    lax.fori_loop(0, rows // ROW_UNROLL, group, 0)


def _start_row_range(idx_ref, src_hbm, dst_ref, sem, base, count):
    for u in range(count):
        _row_copy(src_hbm, dst_ref, sem, idx_ref[0, 0, base + u],
                  (base + u) * SUBLANES).start(priority=1)


def _wait_rows(src_hbm, dst_ref, sem):
    for r in range(dst_ref.shape[0] // SUBLANES):
        _row_copy(src_hbm, dst_ref, sem, 0, r * SUBLANES).wait()


def _ring_step(start, wait):
    i = pl.program_id(0)
    slot = i % 2

    @pl.when(i == 0)
    def _():
        start(0, False)

    @pl.when(i + 1 < pl.num_programs(0))
    def _():
        start(1 - slot, True)

    wait(slot)
    return slot


def _next_block(nb):
    return lambda i: (jnp.minimum(i + 1, nb - 1), 0, 0)


def _combine_kernel(d1_ref, d2_ref, d1n_ref, d2n_ref, w1_ref, w2_ref, h_ref, g_ref, y_hbm, o_ref,
                    y1_ref, y2_ref, sem):
    def start(slot, nxt):
        _start_rows(d1n_ref if nxt else d1_ref, y_hbm, y1_ref.at[slot], sem.at[0, slot])
        _start_rows(d2n_ref if nxt else d2_ref, y_hbm, y2_ref.at[slot], sem.at[1, slot])

    def wait(slot):
        _wait_rows(y_hbm, y1_ref.at[slot], sem.at[0, slot])
        _wait_rows(y_hbm, y2_ref.at[slot], sem.at[1, slot])

    slot = _ring_step(start, wait)
    f = (w1_ref[...] * _load_row_tiles(y1_ref.at[slot])
         + w2_ref[...] * _load_row_tiles(y2_ref.at[slot]))
    o_ref[...] = h_ref[...] + _rms(f, g_ref[...])


def _combine(dest1, dest2, w1, w2, h2d, g, y, tc):
    n, d = h2d.shape
    nb = n // tc
    idx = lambda v: v.reshape(nb, 1, tc)
    smem = pl.BlockSpec((1, 1, tc), lambda i: (i, 0, 0), memory_space=pltpu.SMEM)
    smem_next = pl.BlockSpec((1, 1, tc), _next_block(nb), memory_space=pltpu.SMEM)
    return pl.pallas_call(
        _combine_kernel,
        grid=(nb,),
        in_specs=[smem, smem, smem_next, smem_next,
                  pl.BlockSpec((tc, 1), lambda i: (i, 0)), pl.BlockSpec((tc, 1), lambda i: (i, 0)),
                  pl.BlockSpec((tc, d), lambda i: (i, 0)), pl.BlockSpec((1, d), lambda i: (0, 0)),
                  pl.BlockSpec(memory_space=pl.ANY)],
        out_specs=pl.BlockSpec((tc, d), lambda i: (i, 0)),
        out_shape=jax.ShapeDtypeStruct((n, d), F32),
        scratch_shapes=[pltpu.VMEM((2, tc * SUBLANES, LANES), y.dtype),
                        pltpu.VMEM((2, tc * SUBLANES, LANES), y.dtype),
                        pltpu.SemaphoreType.DMA((2, 2))],
        compiler_params=_cparams(("arbitrary",)),
        name="moe_combine",
    )(idx(dest1), idx(dest2), idx(dest1), idx(dest2), w1, w2, h2d, g, y)


def _moe(hn_tiles, h2d, w_router, wg, wu, wd, g_post, tiles):
    n = hn_tiles.shape[0] // SUBLANES
    tb = tiles["moe_rows"]
    oi, ow, cnt = _route(hn_tiles, w_router.T, tiles["route"])
    e1, e2, r1, r2 = oi[0], oi[1], oi[2], oi[3]
    counts = cnt[:, 0]
    padded = ((counts + tb - 1) // tb) * tb
    pad_end = jnp.cumsum(padded)
    pad_start = pad_end - padded
    dest1 = pad_start[e1] + r1
    dest2 = pad_start[e2] + r2
    n_blocks = -(-(2 * n) // tb) + N_EXPERTS
    tok = jnp.arange(n, dtype=jnp.int32) * SUBLANES
    slot_src = jnp.zeros((n_blocks * tb,), jnp.int32).at[jnp.concatenate([dest1, dest2])].set(
        jnp.concatenate([tok, tok]))
    block_start = jnp.arange(n_blocks, dtype=jnp.int32) * tb
    block_expert = jnp.minimum(jnp.searchsorted(pad_end, block_start, side="right"),
                               N_EXPERTS - 1).astype(jnp.int32)
    y = _ffn_moe(block_expert, slot_src, hn_tiles, wg, wu, wd, tb, tiles["ff"])
    return _combine(dest1 * SUBLANES, dest2 * SUBLANES, ow[0].reshape(n, 1), ow[1].reshape(n, 1),
                    h2d, g_post, y, tiles["combine"])


def _rope_tables(seq):
    t = jnp.arange(seq, dtype=jnp.int32)

    def cs(pos, dim):
        inv = ROPE_THETA ** (-jnp.arange(0, dim, 2, dtype=F32) / dim)
        ang = pos.astype(F32)[:, None] * inv[None, :]
        return jnp.cos(ang), jnp.sin(ang)

    rc, rs = cs(t // GRID_W, HEAD_DIM // 2)
    cc, cs_ = cs(t % GRID_W, HEAD_DIM // 2)
    sc, ss = cs(t, HEAD_DIM)
    cos_a = jnp.concatenate([rc, rc, cc, cc] * 2, axis=1)
    sin_a = jnp.concatenate([-rs, rs, -cs_, cs_] * 2, axis=1)
    cos_b = jnp.concatenate([sc, sc] * 2, axis=1)
    sin_b = jnp.concatenate([-ss, ss] * 2, axis=1)
    return cos_a, sin_a, cos_b, sin_b


def _tiles(n, seq, d_ff):
    pick = lambda total, want: math.gcd(total, want)
    return {
        "rows": pick(seq, 512),
        "tq_a": pick(seq // Q_TILES, 128), "tq_b": pick(seq // Q_TILES, 256), "tk": pick(seq, 512),
        "ffn_rows": pick(n, 1024), "ff": pick(d_ff, 512),
        "moe_rows": 1024 if n >= 4096 else 128, "route": pick(n, 512),
        "combine": pick(n, 256),
    }


def kernel(x, w_in, w_out, g_pre_mix, g_post_mix, g_pre_ffn, g_post_ffn, g_qnorm_a, g_knorm_a, g_out_a, diff_lambda, g_subln_b, w_gate_dense, w_up_dense, w_down_dense, w_router, w_gate_moe, w_up_moe, w_down_moe):
    b, s, d = x.shape
    assert d == D_MODEL, "row-tile layout needs one activation row per (8, 128) tile"
    n = b * s
    depth = w_in.shape[0]
    tiles = _tiles(n, s, w_gate_dense.shape[-1])
    tabs = _rope_tables(s)
    bd = (lax.broadcasted_iota(jnp.int32, (A_Q, A_Q), 0) // HEAD_DIM
          == lax.broadcasted_iota(jnp.int32, (A_Q, A_Q), 1) // HEAD_DIM).astype(BF16)
    row = lambda v: v.reshape(1, -1).astype(F32)

    h = x.reshape(n, d)
    for i in range(depth):
        gq = row(jnp.tile(g_qnorm_a[i], N_HEADS_A)) * Q_SCALE
        gk = row(jnp.tile(g_knorm_a[i], N_KV_A))
        qa, ka, va, qb, kb, vb = _inproj(h, row(g_pre_mix[i]), w_in[i].astype(BF16), gq, gk, tabs, bd,
                                         s, tiles["rows"])
        shp = lambda v: v.reshape(b, s, v.shape[-1])
        oa = _attn_a(shp(qa), shp(ka), shp(va), tiles["tq_a"], tiles["tk"])
        lam_init = 0.8 - 0.6 * math.exp(-0.3 * i)
        ob = _attn_b(diff_lambda[i].astype(F32), row(g_subln_b[i]), shp(qb), shp(kb), shp(vb),
                     tiles["tq_b"], tiles["tk"], lam_init)
        moe_layer = i % 2 == 1
        h, hn = _outproj(oa.reshape(n, A_Q), ob.reshape(n, B_V), h, w_out[i].astype(BF16),
                         row(g_out_a[i]), row(g_post_mix[i]), row(g_pre_ffn[i]),
                         moe_layer, tiles["rows"])
        j = i // 2
        if moe_layer:
            h = _moe(hn, h, w_router[j], w_gate_moe[j].astype(BF16), w_up_moe[j].astype(BF16),
                     w_down_moe[j].astype(BF16), row(g_post_ffn[i]), tiles)
        else:
            h = _ffn_dense(hn, w_gate_dense[j].astype(BF16), w_up_dense[j].astype(BF16),
                           w_down_dense[j].astype(BF16), h, row(g_post_ffn[i]),
                           tiles["ffn_rows"], tiles["ff"])
    return h.reshape(b, s, d)
```

```python
import functools
import math

import jax
import jax.numpy as jnp
from jax import lax
from jax.experimental import pallas as pl
from jax.experimental.pallas import tpu as pltpu

F32 = jnp.float32
BF16 = jnp.bfloat16

HEAD_DIM = 64
N_HEADS_A = 8
N_KV_A = 2
N_HEADS_B = 4
A_Q = N_HEADS_A * HEAD_DIM
A_KV = N_KV_A * HEAD_DIM
B_QK = N_HEADS_B * 2 * HEAD_DIM
B_V = N_HEADS_B * 2 * HEAD_DIM
GRID_W = 64
ROPE_THETA = 10000.0
EPS = 1e-6
N_EXPERTS = 8
LOG2E = 1.4426950408889634
LANES = 128
SUBLANES = 8
D_MODEL = SUBLANES * LANES
Q_SCALE = (HEAD_DIM ** -0.5) * LOG2E

VMEM_LIMIT = 56 * 1024 * 1024
ROW_UNROLL = 8
Q_TILES = 2


def _cparams(sem):
    return pltpu.CompilerParams(dimension_semantics=sem, vmem_limit_bytes=VMEM_LIMIT)


def _rms(x, g):
    return x * lax.rsqrt(jnp.mean(x * x, axis=-1, keepdims=True) + EPS) * g


def _lane(shape):
    return lax.broadcasted_iota(jnp.int32, shape, len(shape) - 1)


def _tile4(t):
    return jnp.concatenate([t, t, t, t], axis=1)


def _store_row_tiles(ref, x):
    rows = x.shape[0]
    for c in range(SUBLANES):
        ref[pl.ds(c, rows, stride=SUBLANES), :] = x[:, c * LANES:(c + 1) * LANES]


def _load_row_tiles(ref):
    rows = ref.shape[0] // SUBLANES
    return jnp.concatenate([ref[pl.ds(c, rows, stride=SUBLANES), :] for c in range(SUBLANES)], axis=1)


def _rotate_pairs(x, half):
    w = x.shape[-1]
    first = (_lane(x.shape) % (2 * half)) < half
    return jnp.where(first, pltpu.roll(x, w - half, axis=1), pltpu.roll(x, half, axis=1))


def _inproj_kernel(h_ref, g_ref, w_ref, gq_ref, gk_ref, ca_ref, sa_ref, cb_ref, sb_ref, bd_ref,
                   qa_ref, ka_ref, va_ref, qb_ref, kb_ref, vb_ref):
    hn = _rms(h_ref[...], g_ref[...]).astype(BF16)
    tm = hn.shape[0]
    lane = _lane((tm, LANES))
    low = lane < HEAD_DIM
    ca, sa = ca_ref[...], sa_ref[...]
    cb, sb = cb_ref[...], sb_ref[...]

    def proj(lo, hi):
        return jnp.dot(hn, w_ref[:, lo:hi], preferred_element_type=F32)

    def head_norm(x, bd, g):
        ms = jnp.dot((x * x).astype(BF16), bd, preferred_element_type=F32) * (1.0 / HEAD_DIM)
        return x * lax.rsqrt(ms + EPS) * g

    qa = head_norm(proj(0, A_Q), bd_ref[...], gq_ref[...])
    qa = qa * _tile4(ca) + _rotate_pairs(qa, HEAD_DIM // 4) * _tile4(sa)
    for i in range(N_HEADS_A // 2):
        slab = qa[:, i * LANES:(i + 1) * LANES]
        qa_ref[:, (2 * i) * LANES:(2 * i + 1) * LANES] = jnp.where(low, slab, 0.0).astype(BF16)
        qa_ref[:, (2 * i + 1) * LANES:(2 * i + 2) * LANES] = jnp.where(
            low, pltpu.roll(slab, HEAD_DIM, axis=1), 0.0).astype(BF16)

    ka = head_norm(proj(A_Q, A_Q + A_KV), bd_ref[0:LANES, 0:LANES], gk_ref[...])
    ka = ka * ca + _rotate_pairs(ka, HEAD_DIM // 4) * sa
    ka_ref[:, 0:LANES] = jnp.where(low, ka, 0.0).astype(BF16)
    ka_ref[:, LANES:2 * LANES] = jnp.where(low, pltpu.roll(ka, HEAD_DIM, axis=1), 0.0).astype(BF16)

    va = proj(A_Q + A_KV, A_Q + 2 * A_KV)
    one_col = jnp.where(lane == HEAD_DIM, 1.0, 0.0)
    va_ref[:, 0:LANES] = jnp.where(low, va, one_col).astype(BF16)
    va_ref[:, LANES:2 * LANES] = jnp.where(low, pltpu.roll(va, HEAD_DIM, axis=1), one_col).astype(BF16)

    o = A_Q + 2 * A_KV
    qb = proj(o, o + B_QK)
    qb = (qb * _tile4(cb) + _rotate_pairs(qb, HEAD_DIM // 2) * _tile4(sb)) * Q_SCALE
    qb_ref[...] = qb.astype(BF16)
    kb = proj(o + B_QK, o + 2 * B_QK)
    kb = kb * _tile4(cb) + _rotate_pairs(kb, HEAD_DIM // 2) * _tile4(sb)
    kb_ref[...] = kb.astype(BF16)
    vb = proj(o + 2 * B_QK, o + 2 * B_QK + B_V)
    first_col = jnp.where(lane == 0, 1.0, 0.0).astype(BF16)
    for hd in range(N_HEADS_B):
        vb_ref[:, (2 * hd) * LANES:(2 * hd + 1) * LANES] = vb[:, hd * LANES:(hd + 1) * LANES].astype(BF16)
        vb_ref[:, (2 * hd + 1) * LANES:(2 * hd + 2) * LANES] = first_col


def _inproj(h2d, g, w_bf, gq, gk, tabs, bd, seq, tm):
    n, d = h2d.shape
    nt = seq // tm
    ca, sa, cb, sb = tabs
    row = lambda i: (i, 0)
    const = lambda i: (0, 0)
    tab = lambda i: (i % nt, 0)
    widths = (N_HEADS_A * LANES, N_KV_A * LANES, N_KV_A * LANES, B_QK, B_QK, 2 * B_V)
    return pl.pallas_call(
        _inproj_kernel,
        grid=(n // tm,),
        in_specs=[pl.BlockSpec((tm, d), row), pl.BlockSpec((1, d), const),
                  pl.BlockSpec(w_bf.shape, const), pl.BlockSpec((1, A_Q), const),
                  pl.BlockSpec((1, A_KV), const)]
                 + [pl.BlockSpec((tm, LANES), tab)] * 4
                 + [pl.BlockSpec(bd.shape, const)],
        out_specs=[pl.BlockSpec((tm, w), row) for w in widths],
        out_shape=[jax.ShapeDtypeStruct((n, w), BF16) for w in widths],
        compiler_params=_cparams(("parallel",)),
        name="inproj",
    )(h2d, g, w_bf, gq, gk, ca, sa, cb, sb, bd)


def _flash(qs, k_ref, v_ref, tk):
    nk = k_ref.shape[0] // tk

    def scores(q, c):
        return lax.dot_general(q, k_ref[c * tk:(c + 1) * tk, :], (((1,), (1,)), ((), ())),
                               preferred_element_type=F32)

    ms = [jnp.full((q.shape[0], 1), -jnp.inf, F32) for q in qs]
    accs = [jnp.zeros((q.shape[0], v_ref.shape[1]), F32) for q in qs]
    ss = [scores(q, 0) for q in qs]
    for c in range(nk):
        nxt = [scores(q, c + 1) if c + 1 < nk else None for q in qs]
        for t, s in enumerate(ss):
            m_new = jnp.maximum(ms[t], jnp.max(s, axis=-1, keepdims=True))
            p = jnp.exp2((s - m_new).astype(BF16))
            accs[t] = jnp.exp2(ms[t] - m_new) * accs[t] + jnp.dot(
                p, v_ref[c * tk:(c + 1) * tk, :], preferred_element_type=F32)
            ms[t] = m_new
        ss = nxt
    return accs


def _attn_a_kernel(q_ref, k_ref, v_ref, o_ref, *, tq, tk):
    g = N_HEADS_A // N_KV_A
    tiles = range(q_ref.shape[0] // tq)
    qs = [jnp.concatenate([q_ref[t * tq:(t + 1) * tq, j * LANES:(j + 1) * LANES] for j in range(g)],
                          axis=0) for t in tiles]
    low = _lane((tq, LANES)) < HEAD_DIM
    for t, acc in zip(tiles, _flash(qs, k_ref, v_ref, tk)):
        o = acc * (1.0 / acc[:, HEAD_DIM:HEAD_DIM + 1])
        slabs = [jnp.where(low, o[(2 * i) * tq:(2 * i + 1) * tq],
                           pltpu.roll(o[(2 * i + 1) * tq:(2 * i + 2) * tq], HEAD_DIM, axis=1))
                 for i in range(g // 2)]
        o_ref[t * tq:(t + 1) * tq, :] = jnp.concatenate(slabs, axis=1).astype(o_ref.dtype)


def _attn_a(qa, ka, va, tq, tk):
    b, s, _ = qa.shape
    g = N_HEADS_A // N_KV_A
    rows = tq * Q_TILES
    return pl.pallas_call(
        functools.partial(_attn_a_kernel, tq=tq, tk=tk),
        grid=(b, N_KV_A, s // rows),
        in_specs=[pl.BlockSpec((None, rows, g * LANES), lambda bi, h, i: (bi, i, h)),
                  pl.BlockSpec((None, s, LANES), lambda bi, h, i: (bi, 0, h)),
                  pl.BlockSpec((None, s, LANES), lambda bi, h, i: (bi, 0, h))],
        out_specs=pl.BlockSpec((None, rows, g * HEAD_DIM), lambda bi, h, i: (bi, i, h)),
        out_shape=jax.ShapeDtypeStruct((b, s, A_Q), BF16),
        compiler_params=_cparams(("parallel", "parallel", "parallel")),
        name="attn_a",
    )(qa, ka, va)


def _attn_b_kernel(lam_ref, gs_ref, q_ref, k_ref, v_ref, o_ref, *, tq, tk, lam_init):
    tiles = range(q_ref.shape[0] // tq)
    low = _lane((tq, LANES)) < HEAD_DIM
    zero = jnp.zeros((tq, LANES), q_ref.dtype)
    qs = []
    for t in tiles:
        q = q_ref[t * tq:(t + 1) * tq, :]
        qs.append(jnp.concatenate([jnp.where(low, q, zero), jnp.where(low, zero, q)], axis=0))
    lp = lam_ref[...]
    lam = (jnp.exp(jnp.sum(lp[0:1] * lp[1:2], axis=-1, keepdims=True))
           - jnp.exp(jnp.sum(lp[2:3] * lp[3:4], axis=-1, keepdims=True)) + lam_init)
    w = 2 * HEAD_DIM
    for t, acc in zip(tiles, _flash(qs, k_ref, v_ref, tk)):
        o = acc[:, 0:w] * (1.0 / acc[:, w:w + 1])
        d = o[0:tq] - lam * o[tq:2 * tq]
        o_ref[t * tq:(t + 1) * tq, :] = (_rms(d, gs_ref[...]) * (1.0 - lam_init)).astype(o_ref.dtype)


def _attn_b(lam_p, gs, qb, kb, vb, tq, tk, lam_init):
    b, s, _ = qb.shape
    rows = tq * Q_TILES
    return pl.pallas_call(
        functools.partial(_attn_b_kernel, tq=tq, tk=tk, lam_init=lam_init),
        grid=(b, N_HEADS_B, s // rows),
        in_specs=[pl.BlockSpec(lam_p.shape, lambda bi, h, i: (0, 0)),
                  pl.BlockSpec(gs.shape, lambda bi, h, i: (0, 0)),
                  pl.BlockSpec((None, rows, LANES), lambda bi, h, i: (bi, i, h)),
                  pl.BlockSpec((None, s, LANES), lambda bi, h, i: (bi, 0, h)),
                  pl.BlockSpec((None, s, 2 * LANES), lambda bi, h, i: (bi, 0, h))],
        out_specs=pl.BlockSpec((None, rows, LANES), lambda bi, h, i: (bi, i, h)),
        out_shape=jax.ShapeDtypeStruct((b, s, B_V), BF16),
        compiler_params=_cparams(("parallel", "parallel", "parallel")),
        name="attn_b",
    )(lam_p, gs, qb, kb, vb)


def _outproj_kernel(oa_ref, ob_ref, h_ref, w_ref, ga_ref, gpost_ref, gpre_ref, h2_ref, hn_ref):
    a = _rms(oa_ref[...].astype(F32), ga_ref[...]).astype(BF16)
    mix = (jnp.dot(a, w_ref[0:A_Q, :], preferred_element_type=F32)
           + jnp.dot(ob_ref[...], w_ref[A_Q:, :], preferred_element_type=F32))
    h2 = h_ref[...] + _rms(mix, gpost_ref[...])
    h2_ref[...] = h2
    hn = _rms(h2, gpre_ref[...])
    if hn_ref.shape[1] == LANES:
        _store_row_tiles(hn_ref, hn)
    else:
        hn_ref[...] = hn.astype(hn_ref.dtype)


def _outproj(oa, ob, h2d, w_bf, ga, gpost, gpre, row_tiles, tm):
    n, d = h2d.shape
    row = lambda i: (i, 0)
    const = lambda i: (0, 0)
    if row_tiles:
        hn_spec = pl.BlockSpec((tm * SUBLANES, LANES), row)
        hn_shape = jax.ShapeDtypeStruct((n * SUBLANES, LANES), F32)
    else:
        hn_spec = pl.BlockSpec((tm, d), row)
        hn_shape = jax.ShapeDtypeStruct((n, d), BF16)
    return pl.pallas_call(
        _outproj_kernel,
        grid=(n // tm,),
        in_specs=[pl.BlockSpec((tm, A_Q), row), pl.BlockSpec((tm, B_V), row),
                  pl.BlockSpec((tm, d), row), pl.BlockSpec(w_bf.shape, const),
                  pl.BlockSpec((1, A_Q), const), pl.BlockSpec((1, d), const),
                  pl.BlockSpec((1, d), const)],
        out_specs=[pl.BlockSpec((tm, d), row), hn_spec],
        out_shape=[jax.ShapeDtypeStruct((n, d), F32), hn_shape],
        compiler_params=_cparams(("parallel",)),
        name="outproj",
    )(oa, ob, h2d, w_bf, ga, gpost, gpre)


def _swiglu_accumulate(x_ref, wg_ref, wu_ref, wd_ref, acc_ref):
    gate = jnp.dot(x_ref[...], wg_ref[...], preferred_element_type=F32)
    up = jnp.dot(x_ref[...], wu_ref[...], preferred_element_type=F32)
    act = (gate * jax.nn.sigmoid(gate) * up).astype(BF16)
    acc_ref[...] += jnp.dot(act, wd_ref[...], preferred_element_type=F32)


def _ffn_dense_kernel(x_ref, wg_ref, wu_ref, wd_ref, h_ref, g_ref, o_ref, acc_ref):
    @pl.when(pl.program_id(1) == 0)
    def _():
        acc_ref[...] = jnp.zeros_like(acc_ref)

    _swiglu_accumulate(x_ref, wg_ref, wu_ref, wd_ref, acc_ref)

    @pl.when(pl.program_id(1) == pl.num_programs(1) - 1)
    def _():
        o_ref[...] = h_ref[...] + _rms(acc_ref[...], g_ref[...])


def _ffn_dense(x_bf, wg, wu, wd, h2d, g, tm, tf):
    n, d = h2d.shape
    f = wg.shape[1]
    return pl.pallas_call(
        _ffn_dense_kernel,
        grid=(n // tm, f // tf),
        in_specs=[pl.BlockSpec((tm, d), lambda i, j: (i, 0)),
                  pl.BlockSpec((d, tf), lambda i, j: (0, j)),
                  pl.BlockSpec((d, tf), lambda i, j: (0, j)),
                  pl.BlockSpec((tf, d), lambda i, j: (j, 0)),
                  pl.BlockSpec((tm, d), lambda i, j: (i, 0)),
                  pl.BlockSpec((1, d), lambda i, j: (0, 0))],
        out_specs=pl.BlockSpec((tm, d), lambda i, j: (i, 0)),
        out_shape=jax.ShapeDtypeStruct((n, d), F32),
        scratch_shapes=[pltpu.VMEM((tm, d), F32)],
        compiler_params=_cparams(("parallel", "arbitrary")),
        name="ffn_dense",
    )(x_bf, wg, wu, wd, h2d, g)


def _ffn_moe_kernel(be_ref, idx_ref, idx_next_ref, idx_next2_ref, src_hbm, wg_ref, wu_ref, wd_ref, o_ref,
                    rows_ref, x_ref, acc_ref, sem, *, n_ff):
    del be_ref
    i, j = pl.program_id(0), pl.program_id(1)
    nb = pl.num_programs(0)
    slot, slot1, slot2 = i % 3, (i + 1) % 3, (i + 2) % 3
    tb = x_ref.shape[0]
    share = rows_ref.shape[1] // (SUBLANES * n_ff)

    @pl.when(jnp.logical_and(i == 0, j == 0))
    def _():
        _start_rows(idx_ref, src_hbm, rows_ref.at[0], sem.at[0])
        _start_rows(idx_next_ref, src_hbm, rows_ref.at[1], sem.at[1])

    @pl.when(j == 0)
    def _():
        _wait_rows(src_hbm, rows_ref.at[slot], sem.at[slot])
        x_ref[...] = _load_row_tiles(rows_ref.at[slot].at[0:tb * SUBLANES]).astype(x_ref.dtype)
        acc_ref[...] = jnp.zeros_like(acc_ref)

    _start_row_range(idx_next2_ref, src_hbm, rows_ref.at[slot2], sem.at[slot2], j * share, share)
    _swiglu_accumulate(x_ref, wg_ref, wu_ref, wd_ref, acc_ref)

    @pl.when(j == n_ff - 1)
    def _():
        _store_row_tiles(o_ref, acc_ref[...])

    @pl.when(jnp.logical_and(i == nb - 1, j == n_ff - 1))
    def _():
        _wait_rows(src_hbm, rows_ref.at[slot1], sem.at[slot1])
        _wait_rows(src_hbm, rows_ref.at[slot2], sem.at[slot2])


def _ffn_moe(block_expert, slot_src, src, wg, wu, wd, tb, tf):
    cap = slot_src.shape[0]
    d, f = wg.shape[1], wg.shape[2]
    nb, n_ff = cap // tb, f // tf
    ring_rows = n_ff * (-(-tb // (n_ff * ROW_UNROLL)) * ROW_UNROLL)
    idx3 = jnp.pad(slot_src.reshape(nb, tb), ((0, 0), (0, ring_rows - tb))).reshape(nb, 1, ring_rows)
    grid_spec = pltpu.PrefetchScalarGridSpec(
        num_scalar_prefetch=1,
        grid=(nb, n_ff),
        in_specs=[pl.BlockSpec((1, 1, ring_rows), lambda i, j, be: (i, 0, 0), memory_space=pltpu.SMEM),
                  pl.BlockSpec((1, 1, ring_rows), lambda i, j, be: (jnp.minimum(i + 1, nb - 1), 0, 0),
                               memory_space=pltpu.SMEM),
                  pl.BlockSpec((1, 1, ring_rows), lambda i, j, be: (jnp.minimum(i + 2, nb - 1), 0, 0),
                               memory_space=pltpu.SMEM),
                  pl.BlockSpec(memory_space=pl.ANY),
                  pl.BlockSpec((None, d, tf), lambda i, j, be: (be[i], 0, j)),
                  pl.BlockSpec((None, d, tf), lambda i, j, be: (be[i], 0, j)),
                  pl.BlockSpec((None, tf, d), lambda i, j, be: (be[i], j, 0))],
        out_specs=pl.BlockSpec((tb * SUBLANES, LANES), lambda i, j, be: (i, 0)),
        scratch_shapes=[pltpu.VMEM((3, ring_rows * SUBLANES, LANES), src.dtype),
                        pltpu.VMEM((tb, d), BF16),
                        pltpu.VMEM((tb, d), F32),
                        pltpu.SemaphoreType.DMA((3,))],
    )
    return pl.pallas_call(
        functools.partial(_ffn_moe_kernel, n_ff=n_ff),
        grid_spec=grid_spec,
        out_shape=jax.ShapeDtypeStruct((cap * SUBLANES, LANES), F32),
        compiler_params=_cparams(("arbitrary", "arbitrary")),
        name="ffn_moe",
    )(block_expert, idx3, idx3, idx3, src, wg, wu, wd)


def _route_kernel(x_ref, wr_ref, tri_ref, oi_ref, ow_ref, cnt_ref, carry_ref):
    @pl.when(pl.program_id(0) == 0)
    def _():
        carry_ref[...] = jnp.zeros_like(carry_ref)

    logits = lax.dot_general(wr_ref[...], _load_row_tiles(x_ref), (((1,), (1,)), ((), ())),
                             precision=lax.Precision.HIGHEST, preferred_element_type=F32)
    shape = logits.shape
    eidx = lax.broadcasted_iota(jnp.int32, shape, 0)
    m1 = jnp.max(logits, axis=0, keepdims=True)
    i1 = jnp.min(jnp.where(logits == m1, eidx, N_EXPERTS), axis=0, keepdims=True)
    oh1 = eidx == i1
    rest = jnp.where(oh1, -jnp.inf, logits)
    m2 = jnp.max(rest, axis=0, keepdims=True)
    i2 = jnp.min(jnp.where(rest == m2, eidx, N_EXPERTS), axis=0, keepdims=True)
    oh2 = eidx == i2
    e = jnp.exp(m2 - m1)
    w1 = 1.0 / (1.0 + e)
    w2 = e * w1

    cnt = jnp.where(oh1, 1.0, jnp.where(oh2, 1.0, 0.0))
    incl = jnp.dot(cnt.astype(BF16), tri_ref[...], preferred_element_type=F32)
    excl = incl - cnt + carry_ref[:, 0:1]
    r1 = jnp.sum(jnp.where(oh1, excl, 0.0), axis=0, keepdims=True).astype(jnp.int32)
    r2 = jnp.sum(jnp.where(oh2, excl, 0.0), axis=0, keepdims=True).astype(jnp.int32)
    carry_ref[...] = carry_ref[...] + jnp.sum(cnt, axis=1, keepdims=True)

    bc = lambda v: jnp.broadcast_to(v, shape)
    oi_ref[...] = jnp.where(eidx == 0, bc(i1), jnp.where(eidx == 1, bc(i2),
                            jnp.where(eidx == 2, bc(r1), bc(r2))))
    ow_ref[...] = jnp.where(eidx == 0, bc(w1), bc(w2))
    cnt_ref[...] = carry_ref[...].astype(jnp.int32)


def _route(hn_tiles, wr_t, tr):
    n = hn_tiles.shape[0] // SUBLANES
    tri = (lax.broadcasted_iota(jnp.int32, (tr, tr), 0)
           <= lax.broadcasted_iota(jnp.int32, (tr, tr), 1)).astype(BF16)
    return pl.pallas_call(
        _route_kernel,
        grid=(n // tr,),
        in_specs=[pl.BlockSpec((tr * SUBLANES, LANES), lambda i: (i, 0)),
                  pl.BlockSpec(wr_t.shape, lambda i: (0, 0)),
                  pl.BlockSpec((tr, tr), lambda i: (0, 0))],
        out_specs=[pl.BlockSpec((N_EXPERTS, tr), lambda i: (0, i)),
                   pl.BlockSpec((N_EXPERTS, tr), lambda i: (0, i)),
                   pl.BlockSpec((N_EXPERTS, LANES), lambda i: (0, 0))],
        out_shape=[jax.ShapeDtypeStruct((N_EXPERTS, n), jnp.int32),
                   jax.ShapeDtypeStruct((N_EXPERTS, n), F32),
                   jax.ShapeDtypeStruct((N_EXPERTS, LANES), jnp.int32)],
        scratch_shapes=[pltpu.VMEM((N_EXPERTS, LANES), F32)],
        compiler_params=_cparams(("arbitrary",)),
        name="route",
    )(hn_tiles, wr_t, tri)


def _row_copy(src_hbm, dst_ref, sem, src_sublane, dst_sublane):
    return pltpu.make_async_copy(src_hbm.at[pl.ds(pl.multiple_of(src_sublane, SUBLANES), SUBLANES), :],
                                 dst_ref.at[pl.ds(pl.multiple_of(dst_sublane, SUBLANES), SUBLANES), :],
                                 sem)


def _start_rows(idx_ref, src_hbm, dst_ref, sem):
    rows = dst_ref.shape[0] // SUBLANES

    def group(g, c):
        base = g * ROW_UNROLL
        for u in range(ROW_UNROLL):
            _row_copy(src_hbm, dst_ref, sem, idx_ref[0, 0, base + u],
                      (base + u) * SUBLANES).start(priority=u % 2)
        return c

    lax.fori_loop(0, rows // ROW_UNROLL, group, 0)


def _start_row_range(idx_ref, src_hbm, dst_ref, sem, base, count):
    for u in range(count):
        _row_copy(src_hbm, dst_ref, sem, idx_ref[0, 0, base + u],
                  (base + u) * SUBLANES).start(priority=u % 2)


def _wait_rows(src_hbm, dst_ref, sem):
    for r in range(dst_ref.shape[0] // SUBLANES):
        _row_copy(src_hbm, dst_ref, sem, 0, r * SUBLANES).wait()


def _ring_step(start, wait):
    i = pl.program_id(0)
    slot = i % 2

    @pl.when(i == 0)
    def _():
        start(0, False)

    @pl.when(i + 1 < pl.num_programs(0))
    def _():
        start(1 - slot, True)

    wait(slot)
    return slot


def _next_block(nb):
    return lambda i: (jnp.minimum(i + 1, nb - 1), 0, 0)


def _combine_kernel(d1_ref, d2_ref, d1n_ref, d2n_ref, w1_ref, w2_ref, h_ref, g_ref, y_hbm, o_ref,
                    y1_ref, y2_ref, sem):
    def start(slot, nxt):
        _start_rows(d1n_ref if nxt else d1_ref, y_hbm, y1_ref.at[slot], sem.at[0, slot])
        _start_rows(d2n_ref if nxt else d2_ref, y_hbm, y2_ref.at[slot], sem.at[1, slot])

    def wait(slot):
        _wait_rows(y_hbm, y1_ref.at[slot], sem.at[0, slot])
        _wait_rows(y_hbm, y2_ref.at[slot], sem.at[1, slot])

    slot = _ring_step(start, wait)
    f = (w1_ref[...] * _load_row_tiles(y1_ref.at[slot])
         + w2_ref[...] * _load_row_tiles(y2_ref.at[slot]))
    o_ref[...] = h_ref[...] + _rms(f, g_ref[...])


def _combine(dest1, dest2, w1, w2, h2d, g, y, tc):
    n, d = h2d.shape
    nb = n // tc
    idx = lambda v: v.reshape(nb, 1, tc)
    smem = pl.BlockSpec((1, 1, tc), lambda i: (i, 0, 0), memory_space=pltpu.SMEM)
    smem_next = pl.BlockSpec((1, 1, tc), _next_block(nb), memory_space=pltpu.SMEM)
    return pl.pallas_call(
        _combine_kernel,
        grid=(nb,),
        in_specs=[smem, smem, smem_next, smem_next,
                  pl.BlockSpec((tc, 1), lambda i: (i, 0)), pl.BlockSpec((tc, 1), lambda i: (i, 0)),
                  pl.BlockSpec((tc, d), lambda i: (i, 0)), pl.BlockSpec((1, d), lambda i: (0, 0)),
                  pl.BlockSpec(memory_space=pl.ANY)],
        out_specs=pl.BlockSpec((tc, d), lambda i: (i, 0)),
        out_shape=jax.ShapeDtypeStruct((n, d), F32),
        scratch_shapes=[pltpu.VMEM((2, tc * SUBLANES, LANES), y.dtype),
                        pltpu.VMEM((2, tc * SUBLANES, LANES), y.dtype),
                        pltpu.SemaphoreType.DMA((2, 2))],
        compiler_params=_cparams(("arbitrary",)),
        name="moe_combine",
    )(idx(dest1), idx(dest2), idx(dest1), idx(dest2), w1, w2, h2d, g, y)


def _moe(hn_tiles, h2d, w_router, wg, wu, wd, g_post, tiles):
    n = hn_tiles.shape[0] // SUBLANES
    tb = tiles["moe_rows"]
    oi, ow, cnt = _route(hn_tiles, w_router.T, tiles["route"])
    e1, e2, r1, r2 = oi[0], oi[1], oi[2], oi[3]
    counts = cnt[:, 0]
    padded = ((counts + tb - 1) // tb) * tb
    pad_end = jnp.cumsum(padded)
    pad_start = pad_end - padded
    dest1 = pad_start[e1] + r1
    dest2 = pad_start[e2] + r2
    n_blocks = -(-(2 * n) // tb) + N_EXPERTS
    tok = jnp.arange(n, dtype=jnp.int32) * SUBLANES
    slot_src = jnp.zeros((n_blocks * tb,), jnp.int32).at[jnp.concatenate([dest1, dest2])].set(
        jnp.concatenate([tok, tok]))
    block_start = jnp.arange(n_blocks, dtype=jnp.int32) * tb
    block_expert = jnp.minimum(jnp.searchsorted(pad_end, block_start, side="right"),
                               N_EXPERTS - 1).astype(jnp.int32)
    y = _ffn_moe(block_expert, slot_src, hn_tiles, wg, wu, wd, tb, tiles["ff"])
    return _combine(dest1 * SUBLANES, dest2 * SUBLANES, ow[0].reshape(n, 1), ow[1].reshape(n, 1),
                    h2d, g_post, y, tiles["combine"])


def _rope_tables(seq):
    t = jnp.arange(seq, dtype=jnp.int32)

    def cs(pos, dim):
        inv = ROPE_THETA ** (-jnp.arange(0, dim, 2, dtype=F32) / dim)
        ang = pos.astype(F32)[:, None] * inv[None, :]
        return jnp.cos(ang), jnp.sin(ang)

    rc, rs = cs(t // GRID_W, HEAD_DIM // 2)
    cc, cs_ = cs(t % GRID_W, HEAD_DIM // 2)
    sc, ss = cs(t, HEAD_DIM)
    cos_a = jnp.concatenate([rc, rc, cc, cc] * 2, axis=1)
    sin_a = jnp.concatenate([-rs, rs, -cs_, cs_] * 2, axis=1)
    cos_b = jnp.concatenate([sc, sc] * 2, axis=1)
    sin_b = jnp.concatenate([-ss, ss] * 2, axis=1)
    return cos_a, sin_a, cos_b, sin_b


def _tiles(n, seq, d_ff):
    pick = lambda total, want: math.gcd(total, want)
    return {
        "rows": pick(seq, 512),
        "tq_a": pick(seq // Q_TILES, 128), "tq_b": pick(seq // Q_TILES, 256), "tk": pick(seq, 512),
        "ffn_rows": pick(n, 1024), "ff": pick(d_ff, 512),
        "moe_rows": 1024 if n >= 4096 else 128, "route": pick(n, 512),
        "combine": pick(n, 256),
    }


def kernel(x, w_in, w_out, g_pre_mix, g_post_mix, g_pre_ffn, g_post_ffn, g_qnorm_a, g_knorm_a, g_out_a, diff_lambda, g_subln_b, w_gate_dense, w_up_dense, w_down_dense, w_router, w_gate_moe, w_up_moe, w_down_moe):
    b, s, d = x.shape
    assert d == D_MODEL, "row-tile layout needs one activation row per (8, 128) tile"
    n = b * s
    depth = w_in.shape[0]
    tiles = _tiles(n, s, w_gate_dense.shape[-1])
    tabs = _rope_tables(s)
    bd = (lax.broadcasted_iota(jnp.int32, (A_Q, A_Q), 0) // HEAD_DIM
          == lax.broadcasted_iota(jnp.int32, (A_Q, A_Q), 1) // HEAD_DIM).astype(BF16)
    row = lambda v: v.reshape(1, -1).astype(F32)

    h = x.reshape(n, d)
    for i in range(depth):
        gq = row(jnp.tile(g_qnorm_a[i], N_HEADS_A)) * Q_SCALE
        gk = row(jnp.tile(g_knorm_a[i], N_KV_A))
        qa, ka, va, qb, kb, vb = _inproj(h, row(g_pre_mix[i]), w_in[i].astype(BF16), gq, gk, tabs, bd,
                                         s, tiles["rows"])
        shp = lambda v: v.reshape(b, s, v.shape[-1])
        oa = _attn_a(shp(qa), shp(ka), shp(va), tiles["tq_a"], tiles["tk"])
        lam_init = 0.8 - 0.6 * math.exp(-0.3 * i)
        ob = _attn_b(diff_lambda[i].astype(F32), row(g_subln_b[i]), shp(qb), shp(kb), shp(vb),
                     tiles["tq_b"], tiles["tk"], lam_init)
        moe_layer = i % 2 == 1
        h, hn = _outproj(oa.reshape(n, A_Q), ob.reshape(n, B_V), h, w_out[i].astype(BF16),
                         row(g_out_a[i]), row(g_post_mix[i]), row(g_pre_ffn[i]),
                         moe_layer, tiles["rows"])
        j = i // 2
        if moe_layer:
            h = _moe(hn, h, w_router[j], w_gate_moe[j].astype(BF16), w_up_moe[j].astype(BF16),
                     w_down_moe[j].astype(BF16), row(g_post_ffn[i]), tiles)
        else:
            h = _ffn_dense(hn, w_gate_dense[j].astype(BF16), w_up_dense[j].astype(BF16),
                           w_down_dense[j].astype(BF16), h, row(g_post_ffn[i]),
                           tiles["ffn_rows"], tiles["ff"])
    return h.reshape(b, s, d)
```
